```python
import jax, jax.numpy as jnp
from jax import lax
import numpy as np

D_MODEL = 1024
BATCH = 4
SEQ = 8192
DEPTH = 1

GDN_H = 4
GDN_DK = 128
GDN_DV = 128
GDN_W = GDN_H * GDN_DV
CONV_W = 4
CHUNK = 64
SB_H = 4
SB_D = 128
SB_W = SB_H * SB_D
Q_BLOCK = 128
MEM_H = 4
MEM_D = 128
MEM_W = MEM_H * MEM_D
N_MEM = 256
N_BRANCH = 3
D_FF = 4 * D_MODEL
EPS = 1e-6
IN_SIZES = (3 * GDN_W, GDN_W, GDN_H, GDN_H, 3 * SB_W, MEM_W, N_BRANCH * D_MODEL)
D_IN = sum(IN_SIZES)

kernel_name = "hybrid_gdn_stickbreak_memxattn_gated_merge"


def rms_norm(x, g):
    xf = x.astype(jnp.float32)
    y = xf * lax.rsqrt(jnp.mean(xf * xf, axis=-1, keepdims=True) + EPS)
    return (y * g.astype(jnp.float32)).astype(x.dtype)


def l2norm(x):
    return x * lax.rsqrt(jnp.sum(x * x, axis=-1, keepdims=True) + EPS)


def causal_dwconv(x, w):
    c = x.shape[-1]
    return lax.conv_general_dilated(
        x, w[:, None, :].astype(x.dtype), window_strides=(1,),
        padding=[(w.shape[0] - 1, 0)], dimension_numbers=("NWC", "WIO", "NWC"),
        feature_group_count=c)


def gated_delta_rule_chunked(q, k, v, g, beta):
    b, s, h, dk = q.shape
    dv = v.shape[-1]
    n = s // CHUNK

    def to_chunks(t):
        return jnp.moveaxis(t.reshape(b, n, CHUNK, h, *t.shape[3:]), 3, 1)

    qc, kc, vc, gc, bc = (to_chunks(t) for t in (q, k, v, g, beta))
    gc = jnp.cumsum(gc, axis=-1)
    idx = jnp.arange(CHUNK)
    causal = idx[:, None] >= idx[None, :]
    strict = idx[:, None] > idx[None, :]
    decay = jnp.exp(jnp.where(causal, gc[..., :, None] - gc[..., None, :], -jnp.inf))
    kb = kc * bc[..., None]
    vb = vc * bc[..., None]
    lower = jnp.where(strict, jnp.einsum("bhncd,bhnsd->bhncs", kb, kc) * decay, 0.0)
    eye = jnp.eye(CHUNK, dtype=jnp.float32)
    t_inv = lax.linalg.triangular_solve(eye + lower, jnp.broadcast_to(eye, lower.shape),
                                        left_side=True, lower=True, unit_diagonal=True)
    u = jnp.einsum("bhncs,bhnsv->bhncv", t_inv, vb)
    w = jnp.einsum("bhncs,bhnsk->bhnck", t_inv, kb * jnp.exp(gc)[..., None])
    a_qk = jnp.where(causal, jnp.einsum("bhncd,bhnsd->bhncs", qc, kc) * decay, 0.0)
    q_dec = qc * jnp.exp(gc)[..., None]
    k_dec = kc * jnp.exp(gc[..., -1:] - gc)[..., None]
    g_last = jnp.exp(gc[..., -1])
    xs = tuple(jnp.moveaxis(t, 2, 0) for t in (u, w, a_qk, q_dec, k_dec, g_last))

    def step(state, inp):
        u_i, w_i, a_i, qd_i, kd_i, gl_i = inp
        v_new = u_i - jnp.einsum("bhck,bhkv->bhcv", w_i, state)
        o_i = (jnp.einsum("bhck,bhkv->bhcv", qd_i, state)
               + jnp.einsum("bhcs,bhsv->bhcv", a_i, v_new))
        state = state * gl_i[..., None, None] + jnp.einsum("bhck,bhcv->bhkv", kd_i, v_new)
        return state, o_i

    s0 = jnp.zeros((b, h, dk, dv), jnp.float32)
    _, o = lax.scan(step, s0, xs)
    o = jnp.moveaxis(jnp.moveaxis(o, 0, 2), 1, 3)
    return o.reshape(b, s, h, dv)


def stick_breaking_attention(q, k, v):
    s = q.shape[2]
    scale = q.shape[-1] ** -0.5
    outs = []
    for blk in range(s // Q_BLOCK):
        q0 = blk * Q_BLOCK
        kl = q0 + Q_BLOCK
        z = jnp.einsum("bhqd,bhkd->bhqk", q[:, :, q0:kl], k[:, :, :kl]) * scale
        mask = jnp.arange(kl)[None, :] < (q0 + jnp.arange(Q_BLOCK))[:, None]
        log_1mb = jnp.where(mask, jax.nn.log_sigmoid(-z), 0.0)
        between = lax.cumsum(log_1mb, axis=3, reverse=True) - log_1mb
        att = jnp.where(mask, jnp.exp(jax.nn.log_sigmoid(z) + between), 0.0)
        outs.append(jnp.einsum("bhqk,bhkd->bhqd", att, v[:, :, :kl]))
    return jnp.concatenate(outs, axis=2)


def setup_inputs(seed: int = 0) -> dict:
    key = jax.random.key(seed)
    ks = jax.random.split(key, 24)
    f32 = jnp.float32
    nrm = lambda k, shape, scale: jax.random.normal(k, shape, f32) * scale
    gain = lambda k, n: 1.0 + 0.02 * jax.random.normal(k, (DEPTH, n), f32)
    dt = jnp.exp(jax.random.uniform(ks[5], (DEPTH, GDN_H), f32, np.log(1e-3), np.log(1e-1)))
    return {
        "x": nrm(ks[0], (BATCH, SEQ, D_MODEL), 1.0),
        "mem": nrm(ks[1], (BATCH, N_MEM, D_MODEL), 1.0),
        "norm1_g": gain(ks[2], D_MODEL),
        "w_in": nrm(ks[3], (DEPTH, D_MODEL, D_IN), D_MODEL ** -0.5),
        "conv_w": nrm(ks[4], (DEPTH, CONV_W, 3 * GDN_W), CONV_W ** -0.5),
        "a_log": jnp.log(jax.random.uniform(ks[6], (DEPTH, GDN_H), f32, 1.0, 16.0)),
        "dt_bias": jnp.log(jnp.expm1(dt)),
        "gdn_norm_g": gain(ks[7], GDN_DV),
        "sb_q_norm_g": gain(ks[8], SB_D),
        "sb_k_norm_g": gain(ks[9], SB_D),
        "mem_norm_g": gain(ks[10], D_MODEL),
        "w_mem_kv": nrm(ks[11], (DEPTH, D_MODEL, 2 * MEM_W), D_MODEL ** -0.5),
        "mem_q_norm_g": gain(ks[12], MEM_D),
        "mem_k_norm_g": gain(ks[13], MEM_D),
        "w_br_gdn": nrm(ks[14], (DEPTH, GDN_W, D_MODEL), GDN_W ** -0.5),
        "w_br_sb": nrm(ks[15], (DEPTH, SB_W, D_MODEL), SB_W ** -0.5),
        "w_br_mem": nrm(ks[16], (DEPTH, MEM_W, D_MODEL), MEM_W ** -0.5),
        "w_o": nrm(ks[17], (DEPTH, D_MODEL, D_MODEL), D_MODEL ** -0.5),
        "norm2_g": gain(ks[18], D_MODEL),
        "w_up": nrm(ks[19], (DEPTH, D_MODEL, D_FF), D_MODEL ** -0.5),
        "w_down": nrm(ks[20], (DEPTH, D_FF, D_MODEL), D_FF ** -0.5),
    }


def reference(x, mem, norm1_g, w_in, conv_w, a_log, dt_bias, gdn_norm_g, sb_q_norm_g, sb_k_norm_g,
              mem_norm_g, w_mem_kv, mem_q_norm_g, mem_k_norm_g, w_br_gdn, w_br_sb, w_br_mem, w_o,
              norm2_g, w_up, w_down):
    b, s, _ = x.shape
    f32 = jnp.float32
    splits = np.cumsum(IN_SIZES)[:-1].tolist()
    for l in range(DEPTH):
        h = rms_norm(x, norm1_g[l])
        proj = h @ w_in[l]
        gdn_qkv, gdn_z, gdn_a, gdn_b, sb_qkv, mem_q, gate_logits = jnp.split(proj, splits, axis=-1)

        gdn_qkv = jax.nn.silu(causal_dwconv(gdn_qkv, conv_w[l])).astype(f32)
        gq, gk, gv = jnp.split(gdn_qkv, 3, axis=-1)
        gq = l2norm(gq.reshape(b, s, GDN_H, GDN_DK)) * (GDN_DK ** -0.5)
        gk = l2norm(gk.reshape(b, s, GDN_H, GDN_DK))
        gv = gv.reshape(b, s, GDN_H, GDN_DV)
        beta = jax.nn.sigmoid(gdn_b.astype(f32))
        g = -jnp.exp(a_log[l].astype(f32)) * jax.nn.softplus(gdn_a.astype(f32) + dt_bias[l].astype(f32))
        o_gdn = gated_delta_rule_chunked(gq, gk, gv, g, beta)
        o_gdn = rms_norm(o_gdn, gdn_norm_g[l]) * jax.nn.silu(gdn_z.astype(f32).reshape(b, s, GDN_H, GDN_DV))
        y_gdn = o_gdn.reshape(b, s, GDN_W).astype(x.dtype) @ w_br_gdn[l]

        sq, sk, sv = jnp.split(sb_qkv, 3, axis=-1)
        sq = rms_norm(sq.reshape(b, s, SB_H, SB_D), sb_q_norm_g[l])
        sk = rms_norm(sk.reshape(b, s, SB_H, SB_D), sb_k_norm_g[l])
        sv = sv.reshape(b, s, SB_H, SB_D)
        to_bhsd = lambda t: jnp.transpose(t, (0, 2, 1, 3)).astype(f32)
        o_sb = stick_breaking_attention(to_bhsd(sq), to_bhsd(sk), to_bhsd(sv))
        y_sb = jnp.transpose(o_sb, (0, 2, 1, 3)).reshape(b, s, SB_W).astype(x.dtype) @ w_br_sb[l]

        kv = rms_norm(mem, mem_norm_g[l]) @ w_mem_kv[l]
        km, vm = jnp.split(kv, 2, axis=-1)
        km = rms_norm(km.reshape(b, N_MEM, MEM_H, MEM_D), mem_k_norm_g[l]).astype(f32)
        vm = vm.reshape(b, N_MEM, MEM_H, MEM_D).astype(f32)
        qm = rms_norm(mem_q.reshape(b, s, MEM_H, MEM_D), mem_q_norm_g[l]).astype(f32)
        p = jax.nn.softmax(jnp.einsum("bshd,bmhd->bhsm", qm, km) * (MEM_D ** -0.5), axis=-1)
        o_mem = jnp.einsum("bhsm,bmhd->bshd", p, vm).reshape(b, s, MEM_W).astype(x.dtype)
        y_mem = o_mem @ w_br_mem[l]

        g_gdn, g_sb, g_mem = jnp.split(jax.nn.sigmoid(gate_logits), N_BRANCH, axis=-1)
        mix = g_gdn * y_gdn + g_sb * y_sb + g_mem * y_mem
        x = x + mix @ w_o[l]

        h2 = rms_norm(x, norm2_g[l])
        x = x + jnp.square(jax.nn.relu(h2 @ w_up[l])) @ w_down[l]
    return x
```

```python
import functools

import jax
import jax.numpy as jnp
from jax import lax
from jax.experimental import pallas as pl
from jax.experimental.pallas import tpu as pltpu

F32 = jnp.float32
BF16 = jnp.bfloat16

EPS = 1e-6
N_HEAD = 4
HEAD_D = 128
BR_W = N_HEAD * HEAD_D
CONV_W = 4
CHUNK = 64
LANE = 128
V7X_VMEM_BYTES = 64 * 1024 * 1024
EXP_ZERO_BELOW = -104.0


def _vmem_limit(estimate_bytes):
    return int(min(estimate_bytes * 1.25 + (4 << 20), V7X_VMEM_BYTES - (6 << 20)))


def _dot(a, b):
    return jnp.dot(a, b, preferred_element_type=F32)


def _dot_nt(a, b):
    return lax.dot_general(a, b, (((1,), (1,)), ((), ())), preferred_element_type=F32)


def _dot_tn(a, b):
    return lax.dot_general(a, b, (((0,), (0,)), ((), ())), preferred_element_type=F32)


def _split(a):
    hi = a.astype(BF16)
    lo = (a - hi.astype(F32)).astype(BF16)
    return hi, lo


def _dot_exact_lhs(m_bf16, b):
    hi, lo = _split(b)
    return _dot(m_bf16, hi) + _dot(m_bf16, lo)


def _dot_exact_rhs(a, m_bf16):
    hi, lo = _split(a)
    return _dot(hi, m_bf16) + _dot(lo, m_bf16)


def _dot3(a, b):
    ah, al = _split(a)
    bh, bl = _split(b)
    return _dot(ah, bh) + (_dot(ah, bl) + _dot(al, bh))


def _silu(x):
    return x * jax.nn.sigmoid(x)


def _softplus(x):
    return jnp.maximum(x, 0.0) + jnp.log1p(jnp.exp(-jnp.abs(x)))


def _rms_rows(x, g_row):
    ms = jnp.mean(x * x, axis=-1, keepdims=True)
    return x * lax.rsqrt(ms + EPS) * g_row


IN_COL_CHUNK = 512


def _in_proj_kernel(sizes, x_ref, g_ref, w_ref, *out_refs):
    hn = _rms_rows(x_ref[...], g_ref[...]).astype(BF16)
    col = 0
    for o_ref, n in zip(out_refs, sizes):
        for c0 in range(0, n, IN_COL_CHUNK):
            cw = min(IN_COL_CHUNK, n - c0)
            o_ref[:, c0:c0 + cw] = _dot(hn, w_ref[:, col + c0:col + c0 + cw])
        col += n


def _in_proj(x2, g, w, sizes, tm):
    t, d = x2.shape
    n_all = w.shape[1]
    est = 2 * tm * d * 4 + d * n_all * 2 + 2 * tm * n_all * 4
    return pl.pallas_call(
        functools.partial(_in_proj_kernel, sizes),
        grid=(t // tm,),
        in_specs=[
            pl.BlockSpec((tm, d), lambda i: (i, 0)),
            pl.BlockSpec((1, d), lambda i: (0, 0)),
            pl.BlockSpec((d, n_all), lambda i: (0, 0), pipeline_mode=pl.Buffered(1)),
        ],
        out_specs=[pl.BlockSpec((tm, n), lambda i: (i, 0)) for n in sizes],
        out_shape=[jax.ShapeDtypeStruct((t, n), F32) for n in sizes],
        compiler_params=pltpu.CompilerParams(
            dimension_semantics=("arbitrary",), vmem_limit_bytes=_vmem_limit(est)),
        name="in_proj",
    )(x2, g, w)


def _tri_inverse(lower, m16, m32):
    n = lower.shape[0]
    eye = (lax.broadcasted_iota(jnp.int32, (n, n), 0)
           == lax.broadcasted_iota(jnp.int32, (n, n), 1)).astype(F32)
    ld = jnp.where(m16, lower, 0.0)
    a2 = _dot3(ld, ld)
    p = eye - ld
    p = p + _dot3(p, a2)
    a4 = _dot3(a2, a2)
    p = p + _dot3(p, a4)
    a8 = _dot3(a4, a4)
    p = p + _dot3(p, a8)
    c32 = jnp.where(m32 & jnp.logical_not(m16), lower, 0.0)
    p = p - _dot3(p, _dot3(c32, p))
    c64 = jnp.where(m32, 0.0, lower)
    p = p - _dot3(p, _dot3(c64, p))
    return p


def _gdn_kernel(ts, qkv_ref, z_ref, ab_ref, cw_ref, gp_ref, ng_ref, o_ref,
                xp_ref, q_s, k_s, v_s, g_s, b_s, state_ref):
    s_blk = pl.program_id(1)
    w3 = 3 * BR_W

    @pl.when(s_blk == 0)
    def _():
        xp_ref[0:8, :] = jnp.zeros((8, w3), F32)
        state_ref[...] = jnp.zeros_like(state_ref)

    xp_ref[8:8 + ts, :] = qkv_ref[...]
    cw = cw_ref[...]
    acc = cw[0:1, :] * xp_ref[pl.ds(8 - (CONV_W - 1), ts), :]
    for j in range(1, CONV_W):
        acc = acc + cw[j:j + 1, :] * xp_ref[pl.ds(8 - (CONV_W - 1) + j, ts), :]
    xp_ref[0:8, :] = xp_ref[ts:ts + 8, :]
    act = _silu(acc)

    ab = ab_ref[...]
    g_all = -jnp.exp(gp_ref[0:1, :]) * _softplus(ab + gp_ref[1:2, :])
    beta_all = jax.nn.sigmoid(ab)
    for h in range(N_HEAD):
        sl = slice(h * HEAD_D, (h + 1) * HEAD_D)
        q = act[:, h * HEAD_D:(h + 1) * HEAD_D]
        k = act[:, BR_W + h * HEAD_D:BR_W + (h + 1) * HEAD_D]
        q_s[:, sl] = q * lax.rsqrt(jnp.sum(q * q, axis=-1, keepdims=True) + EPS) * (HEAD_D ** -0.5)
        k_s[:, sl] = k * lax.rsqrt(jnp.sum(k * k, axis=-1, keepdims=True) + EPS)
        g_s[:, sl] = jnp.broadcast_to(g_all[:, h:h + 1], (ts, HEAD_D))
        b_s[:, sl] = jnp.broadcast_to(beta_all[:, N_HEAD + h:N_HEAD + h + 1], (ts, HEAD_D))
    v_s[...] = act[:, 2 * BR_W:3 * BR_W]

    row = lax.broadcasted_iota(jnp.int32, (CHUNK, CHUNK), 0)
    col = lax.broadcasted_iota(jnp.int32, (CHUNK, CHUNK), 1)
    causal = row >= col
    strict = row > col
    tri = causal.astype(BF16)
    m16 = (row // 16) == (col // 16)
    m32 = (row // 32) == (col // 32)
    ng = ng_ref[...]

    def chunk_body(c, carry):
        r0 = pl.multiple_of(c * CHUNK, CHUNK)
        rows = pl.ds(r0, CHUNK)
        for h in range(N_HEAD):
            sl = slice(h * HEAD_D, (h + 1) * HEAD_D)
            q = q_s[rows, sl]
            k = k_s[rows, sl]
            v = v_s[rows, sl]
            gl = g_s[rows, sl]
            beta = b_s[rows, sl]
            gcb = _dot_exact_lhs(tri, gl)
            d = _dot_exact_lhs(tri, jnp.where(strict, gl[:, :CHUNK], 0.0))
            decay = jnp.exp(d)
            eg = jnp.exp(gcb)
            g_last = gcb[CHUNK - 1:CHUNK, :]
            kb = k * beta
            vb = v * beta
            k16 = k.astype(BF16)
            lower = jnp.where(strict, _dot_nt(kb.astype(BF16), k16) * decay, 0.0)
            a_qk = jnp.where(causal, _dot_nt(q.astype(BF16), k16) * decay, 0.0)
            t_inv = _tri_inverse(lower, m16, m32).astype(BF16)
            u = _dot(t_inv, vb.astype(BF16))
            w = _dot(t_inv, (kb * eg).astype(BF16))
            q_dec = (q * eg).astype(BF16)
            k_dec = (k * jnp.exp(g_last - gcb)).astype(BF16)
            st = state_ref[h]
            st16 = st.astype(BF16)
            v_new = u - _dot(w.astype(BF16), st16)
            v_new16 = v_new.astype(BF16)
            o = _dot(q_dec, st16) + _dot(a_qk.astype(BF16), v_new16)
            state_ref[h] = st * jnp.exp(g_last) + _dot_tn(k_dec, v_new16)
            zz = z_ref[rows, sl]
            o_ref[rows, sl] = (_rms_rows(o, ng) * _silu(zz)).astype(o_ref.dtype)
        return carry

    lax.fori_loop(0, ts // CHUNK, chunk_body, 0)


def _gdn(gqkv, gz, gab, conv_w, gp, ng, b, s, ts):
    w3 = 3 * BR_W
    nsb = s // ts
    est = (2 * ts * (w3 + BR_W + LANE) * 4 + 2 * ts * BR_W * 2 + (ts + 16) * w3 * 4
           + 5 * ts * BR_W * 4 + N_HEAD * HEAD_D * HEAD_D * 4 + 6 * ts * w3 * 4)
    return pl.pallas_call(
        functools.partial(_gdn_kernel, ts),
        grid=(b, nsb),
        in_specs=[
            pl.BlockSpec((ts, w3), lambda bi, si: (bi * nsb + si, 0)),
            pl.BlockSpec((ts, BR_W), lambda bi, si: (bi * nsb + si, 0)),
            pl.BlockSpec((ts, LANE), lambda bi, si: (bi * nsb + si, 0)),
            pl.BlockSpec((CONV_W, w3), lambda bi, si: (0, 0)),
            pl.BlockSpec((8, LANE), lambda bi, si: (0, 0)),
            pl.BlockSpec((1, HEAD_D), lambda bi, si: (0, 0)),
        ],
        out_specs=pl.BlockSpec((ts, BR_W), lambda bi, si: (bi * nsb + si, 0)),
        out_shape=jax.ShapeDtypeStruct((b * s, BR_W), BF16),
        scratch_shapes=[
            pltpu.VMEM((ts + 16, w3), F32),
            pltpu.VMEM((ts, BR_W), F32),
            pltpu.VMEM((ts, BR_W), F32),
            pltpu.VMEM((ts, BR_W), F32),
            pltpu.VMEM((ts, BR_W), F32),
            pltpu.VMEM((ts, BR_W), F32),
            pltpu.VMEM((N_HEAD, HEAD_D, HEAD_D), F32),
        ],
        compiler_params=pltpu.CompilerParams(
            dimension_semantics=("arbitrary", "arbitrary"), vmem_limit_bytes=_vmem_limit(est)),
        name="gdn",
    )(gqkv, gz, gab, conv_w, gp, ng)


def _sb_kernel(tq, q_ref, k_ref, v_ref, gq_ref, gk_ref, o_ref, acc_ref, c_ref):
    i = pl.program_id(2)
    scale = HEAD_D ** -0.5
    qn = _rms_rows(q_ref[...], gq_ref[...]).astype(BF16)
    acc_ref[...] = jnp.zeros_like(acc_ref)
    c_ref[...] = jnp.zeros_like(c_ref)
    row = lax.broadcasted_iota(jnp.int32, (tq, tq), 0)
    col = lax.broadcasted_iota(jnp.int32, (tq, tq), 1)
    later = (row > col).astype(BF16)
    gk = gk_ref[...]

    def cond(carry):
        kb, cmax = carry
        return jnp.logical_and(kb >= 0, cmax > EXP_ZERO_BELOW)

    def body(carry):
        kb, _ = carry
        k0 = pl.multiple_of(kb * tq, tq)
        kn = _rms_rows(k_ref[pl.ds(k0, tq), :], gk).astype(BF16)
        vv = v_ref[pl.ds(k0, tq), :].astype(BF16)
        z = _dot_nt(qn, kn) * scale
        mask = (col + k0) < (row + i * tq)
        log_1mb = jnp.where(mask, -_softplus(z), 0.0)
        c = c_ref[...]
        between = c + _dot_exact_rhs(log_1mb, later)
        att = jnp.where(mask, jnp.exp(z + log_1mb + between), 0.0)
        acc_ref[...] += _dot(att.astype(BF16), vv)
        c_new = c + jnp.sum(log_1mb, axis=-1, keepdims=True)
        c_ref[...] = c_new
        return kb - 1, jnp.max(c_new)

    lax.while_loop(cond, body, (i, jnp.float32(0.0)))
    o_ref[...] = acc_ref[...].astype(o_ref.dtype)


def _sb_attn(sqkv3, gq, gk, b, s, tq):
    nq = s // tq
    est = 2 * tq * HEAD_D * 4 + 4 * s * HEAD_D * 4 + 2 * tq * HEAD_D * 2 + 12 * tq * tq * 4
    return pl.pallas_call(
        functools.partial(_sb_kernel, tq),
        grid=(b, N_HEAD, nq),
        in_specs=[
            pl.BlockSpec((None, tq, HEAD_D), lambda bi, h, i: (bi, i, h)),
            pl.BlockSpec((None, s, HEAD_D), lambda bi, h, i: (bi, 0, N_HEAD + h)),
            pl.BlockSpec((None, s, HEAD_D), lambda bi, h, i: (bi, 0, 2 * N_HEAD + h)),
            pl.BlockSpec((1, HEAD_D), lambda bi, h, i: (0, 0)),
            pl.BlockSpec((1, HEAD_D), lambda bi, h, i: (0, 0)),
        ],
        out_specs=pl.BlockSpec((None, tq, HEAD_D), lambda bi, h, i: (bi, i, h)),
        out_shape=jax.ShapeDtypeStruct((b, s, BR_W), BF16),
        scratch_shapes=[pltpu.VMEM((tq, HEAD_D), F32), pltpu.VMEM((tq, 1), F32)],
        compiler_params=pltpu.CompilerParams(
            dimension_semantics=("arbitrary", "arbitrary", "arbitrary"),
            vmem_limit_bytes=_vmem_limit(est)),
        name="sb_attn",
    )(sqkv3, sqkv3, sqkv3, gq, gk)


def _mem_kv_kernel(mem_ref, g_ref, w_ref, gk_ref, k_ref, v_ref):
    hn = _rms_rows(mem_ref[...], g_ref[...]).astype(BF16)
    kv = _dot(hn, w_ref[...])
    gk = gk_ref[...]
    for h in range(N_HEAD):
        sl = slice(h * HEAD_D, (h + 1) * HEAD_D)
        k_ref[:, sl] = _rms_rows(kv[:, sl], gk).astype(k_ref.dtype)
    v_ref[...] = kv[:, BR_W:].astype(v_ref.dtype)


def _mem_kv(mem, g, w, gk):
    b, m, d = mem.shape
    est = 2 * m * d * 4 + 2 * d * 2 * BR_W * 2 + 4 * m * BR_W * 2 + 4 * m * 2 * BR_W * 4
    return pl.pallas_call(
        _mem_kv_kernel,
        grid=(b,),
        in_specs=[
            pl.BlockSpec((None, m, d), lambda bi: (bi, 0, 0)),
            pl.BlockSpec((1, d), lambda bi: (0, 0)),
            pl.BlockSpec((d, 2 * BR_W), lambda bi: (0, 0)),
            pl.BlockSpec((1, HEAD_D), lambda bi: (0, 0)),
        ],
        out_specs=[pl.BlockSpec((None, m, BR_W), lambda bi: (bi, 0, 0))] * 2,
        out_shape=[jax.ShapeDtypeStruct((b, m, BR_W), BF16)] * 2,
        compiler_params=pltpu.CompilerParams(
            dimension_semantics=("arbitrary",), vmem_limit_bytes=_vmem_limit(est)),
        name="mem_kv",
    )(mem, g, w, gk)


def _mem_attn_kernel(q_ref, k_ref, v_ref, gq_ref, o_ref):
    scale = HEAD_D ** -0.5
    gq = gq_ref[...]
    for h in range(N_HEAD):
        sl = slice(h * HEAD_D, (h + 1) * HEAD_D)
        qn = _rms_rows(q_ref[:, sl], gq).astype(BF16)
        sc = _dot_nt(qn, k_ref[:, sl]) * scale
        e = jnp.exp(sc - jnp.max(sc, axis=-1, keepdims=True))
        o = _dot(e.astype(BF16), v_ref[:, sl])
        o_ref[:, sl] = (o / jnp.sum(e, axis=-1, keepdims=True)).astype(o_ref.dtype)


def _mem_attn(mq3, km, vm, gq, ts):
    b, s, _ = mq3.shape
    m = km.shape[1]
    est = 2 * ts * BR_W * 4 + 4 * m * BR_W * 2 + 2 * ts * BR_W * 2 + 6 * ts * m * 4
    return pl.pallas_call(
        _mem_attn_kernel,
        grid=(b, s // ts),
        in_specs=[
            pl.BlockSpec((None, ts, BR_W), lambda bi, si: (bi, si, 0)),
            pl.BlockSpec((None, m, BR_W), lambda bi, si: (bi, 0, 0)),
            pl.BlockSpec((None, m, BR_W), lambda bi, si: (bi, 0, 0)),
            pl.BlockSpec((1, HEAD_D), lambda bi, si: (0, 0)),
        ],
        out_specs=pl.BlockSpec((None, ts, BR_W), lambda bi, si: (bi, si, 0)),
        out_shape=jax.ShapeDtypeStruct((b, s, BR_W), BF16),
        compiler_params=pltpu.CompilerParams(
            dimension_semantics=("arbitrary", "arbitrary"), vmem_limit_bytes=_vmem_limit(est)),
        name="mem_attn",
    )(mq3, km, vm, gq)


def _merge_kernel(x_ref, gate_ref, og_ref, os_ref, om_ref, wg_ref, ws_ref, wm_ref, wo_ref, o_ref):
    d = x_ref.shape[1]
    mix = None
    for j, (b_ref, w_ref) in enumerate(((og_ref, wg_ref), (os_ref, ws_ref), (om_ref, wm_ref))):
        y = _dot(b_ref[...], w_ref[...])
        term = jax.nn.sigmoid(gate_ref[:, j * d:(j + 1) * d]) * y
        mix = term if mix is None else mix + term
    o_ref[...] = x_ref[...] + _dot(mix.astype(BF16), wo_ref[...])


def _merge(x2, gates, og, osb, om, wg, ws, wm, wo, tm):
    t, d = x2.shape
    est = (2 * tm * d * 4 + 2 * tm * 3 * d * 4 + 6 * tm * BR_W * 2 + 3 * BR_W * d * 2 + d * d * 2
           + 2 * tm * d * 4 + 4 * tm * d * 4)
    wspec = lambda shape: pl.BlockSpec(shape, lambda i: (0, 0), pipeline_mode=pl.Buffered(1))
    return pl.pallas_call(
        _merge_kernel,
        grid=(t // tm,),
        in_specs=[
            pl.BlockSpec((tm, d), lambda i: (i, 0)),
            pl.BlockSpec((tm, 3 * d), lambda i: (i, 0)),
            pl.BlockSpec((tm, BR_W), lambda i: (i, 0)),
            pl.BlockSpec((tm, BR_W), lambda i: (i, 0)),
            pl.BlockSpec((tm, BR_W), lambda i: (i, 0)),
            wspec((BR_W, d)), wspec((BR_W, d)), wspec((BR_W, d)), wspec((d, d)),
        ],
        out_specs=pl.BlockSpec((tm, d), lambda i: (i, 0)),
        out_shape=jax.ShapeDtypeStruct((t, d), F32),
        compiler_params=pltpu.CompilerParams(
            dimension_semantics=("arbitrary",), vmem_limit_bytes=_vmem_limit(est)),
        name="merge",
    )(x2, gates, og, osb, om, wg, ws, wm, wo)


FF_CHUNK = 1024


def _mlp_kernel(x_ref, g_ref, wu_ref, wd_ref, o_ref):
    x = x_ref[...]
    hn = _rms_rows(x, g_ref[...]).astype(BF16)
    acc = x
    for c0 in range(0, wu_ref.shape[1], FF_CHUNK):
        up = _dot(hn, wu_ref[:, c0:c0 + FF_CHUNK])
        act = jnp.square(jnp.maximum(up, 0.0)).astype(BF16)
        acc = acc + _dot(act, wd_ref[c0:c0 + FF_CHUNK, :])
    o_ref[...] = acc


def _mlp(x2, g, wu, wd, tm):
    t, d = x2.shape
    ff = wu.shape[1]
    est = 4 * tm * d * 4 + 2 * d * ff * 2 + 3 * tm * FF_CHUNK * 4 + 2 * tm * d * 4
    wspec = lambda shape: pl.BlockSpec(shape, lambda i: (0, 0), pipeline_mode=pl.Buffered(1))
    return pl.pallas_call(
        _mlp_kernel,
        grid=(t // tm,),
        in_specs=[
            pl.BlockSpec((tm, d), lambda i: (i, 0)),
            pl.BlockSpec((1, d), lambda i: (0, 0)),
            wspec((d, ff)), wspec((ff, d)),
        ],
        out_specs=pl.BlockSpec((tm, d), lambda i: (i, 0)),
        out_shape=jax.ShapeDtypeStruct((t, d), F32),
        compiler_params=pltpu.CompilerParams(
            dimension_semantics=("arbitrary",), vmem_limit_bytes=_vmem_limit(est)),
        name="mlp",
    )(x2, g, wu, wd)


def _pick(n, pref):
    t = min(n, pref)
    while n % t:
        t //= 2
    return t


def _layer(x, mem, norm1_g, w_in, conv_w, a_log, dt_bias, gdn_norm_g, sb_q_norm_g, sb_k_norm_g,
           mem_norm_g, w_mem_kv, mem_q_norm_g, mem_k_norm_g, w_br_gdn, w_br_sb, w_br_mem, w_o,
           norm2_g, w_up, w_down):
    b, s, d = x.shape
    t = b * s
    x2 = x.reshape(t, d)
    row = lambda v: v.reshape(1, -1).astype(F32)

    c = [0, 3 * BR_W, 4 * BR_W, 4 * BR_W + N_HEAD, 4 * BR_W + 2 * N_HEAD,
         7 * BR_W + 2 * N_HEAD, 8 * BR_W + 2 * N_HEAD, 8 * BR_W + 2 * N_HEAD + 3 * d]
    w_ab = jnp.pad(w_in[:, c[2]:c[4]], ((0, 0), (0, LANE - 2 * N_HEAD)))
    w_all = jnp.concatenate(
        [w_in[:, c[6]:c[7]], w_in[:, c[0]:c[1]], w_in[:, c[4]:c[5]], w_in[:, c[1]:c[2]],
         w_in[:, c[5]:c[6]], w_ab], axis=1).astype(BF16)
    sizes = (3 * d, 3 * BR_W, 3 * BR_W, BR_W, BR_W, LANE)
    gates, gqkv, sqkv, gz, mq, gab = _in_proj(x2, row(norm1_g), w_all, sizes, _pick(t, 256))

    gp = jnp.zeros((8, LANE), F32)
    gp = gp.at[0, :N_HEAD].set(a_log.astype(F32)).at[1, :N_HEAD].set(dt_bias.astype(F32))
    o_gdn = _gdn(gqkv, gz, gab, conv_w.astype(F32), gp, row(gdn_norm_g), b, s, _pick(s, 512))

    o_sb = _sb_attn(sqkv.reshape(b, s, 3 * BR_W), row(sb_q_norm_g), row(sb_k_norm_g), b, s,
                    _pick(s, 256))

    km, vm = _mem_kv(mem, row(mem_norm_g), w_mem_kv.astype(BF16), row(mem_k_norm_g))
    o_mem = _mem_attn(mq.reshape(b, s, BR_W), km, vm, row(mem_q_norm_g), _pick(s, 512))

    x1 = _merge(x2, gates, o_gdn, o_sb.reshape(t, BR_W), o_mem.reshape(t, BR_W),
                w_br_gdn.astype(BF16), w_br_sb.astype(BF16), w_br_mem.astype(BF16),
                w_o.astype(BF16), _pick(t, 512))
    out = _mlp(x1, row(norm2_g), w_up.astype(BF16), w_down.astype(BF16), _pick(t, 512))
    return out.reshape(b, s, d)


def kernel(x, mem, norm1_g, w_in, conv_w, a_log, dt_bias, gdn_norm_g, sb_q_norm_g, sb_k_norm_g,
           mem_norm_g, w_mem_kv, mem_q_norm_g, mem_k_norm_g, w_br_gdn, w_br_sb, w_br_mem, w_o,
           norm2_g, w_up, w_down):
    depth = w_in.shape[0]
    for l in range(depth):
        x = _layer(x, mem, norm1_g[l], w_in[l], conv_w[l], a_log[l], dt_bias[l], gdn_norm_g[l],
                   sb_q_norm_g[l], sb_k_norm_g[l], mem_norm_g[l], w_mem_kv[l], mem_q_norm_g[l],
                   mem_k_norm_g[l], w_br_gdn[l], w_br_sb[l], w_br_mem[l], w_o[l], norm2_g[l],
                   w_up[l], w_down[l])
    return x
```

```python
import functools

import jax
import jax.numpy as jnp
from jax import lax
from jax.experimental import pallas as pl
from jax.experimental.pallas import tpu as pltpu

F32 = jnp.float32
BF16 = jnp.bfloat16

EPS = 1e-6
N_HEAD = 4
HEAD_D = 128
BR_W = N_HEAD * HEAD_D
CONV_W = 4
CHUNK = 64
LANE = 128
V7X_VMEM_BYTES = 64 * 1024 * 1024
EXP_ZERO_BELOW = -104.0


def _vmem_limit(estimate_bytes):
    return int(min(estimate_bytes * 1.25 + (4 << 20), V7X_VMEM_BYTES - (6 << 20)))


def _dot(a, b):
    return jnp.dot(a, b, preferred_element_type=F32)


def _dot_nt(a, b):
    return lax.dot_general(a, b, (((1,), (1,)), ((), ())), preferred_element_type=F32)


def _dot_tn(a, b):
    return lax.dot_general(a, b, (((0,), (0,)), ((), ())), preferred_element_type=F32)


def _split(a):
    hi = a.astype(BF16)
    lo = (a - hi.astype(F32)).astype(BF16)
    return hi, lo


def _dot_exact_lhs(m_bf16, b):
    hi, lo = _split(b)
    return _dot(m_bf16, hi) + _dot(m_bf16, lo)


def _dot_exact_rhs(a, m_bf16):
    hi, lo = _split(a)
    return _dot(hi, m_bf16) + _dot(lo, m_bf16)


def _dot3(a, b):
    ah, al = _split(a)
    bh, bl = _split(b)
    return _dot(ah, bh) + (_dot(ah, bl) + _dot(al, bh))


def _silu(x):
    return (0.5 * x) * (1.0 + jnp.tanh(0.5 * x))


def _softplus(x):
    return jnp.maximum(x, 0.0) + jnp.log1p(jnp.exp(-jnp.abs(x)))


def _rms_rows(x, g_row):
    ms = jnp.mean(x * x, axis=-1, keepdims=True)
    return x * lax.rsqrt(ms + EPS) * g_row


IN_COL_CHUNK = 512


def _in_proj_kernel(outs, x_ref, g_ref, hg_ref, w_ref, *out_refs):
    hn = _rms_rows(x_ref[...], g_ref[...]).astype(BF16)
    col = 0
    for o_ref, (n, dtype, gain_row) in zip(out_refs, outs):
        for c0 in range(0, n, IN_COL_CHUNK):
            cw = min(IN_COL_CHUNK, n - c0)
            r = _dot(hn, w_ref[:, col + c0:col + c0 + cw])
            if gain_row is None:
                o_ref[:, c0:c0 + cw] = r.astype(dtype)
            else:
                gain = hg_ref[gain_row:gain_row + 1, :]
                for h0 in range(0, cw, HEAD_D):
                    o_ref[:, c0 + h0:c0 + h0 + HEAD_D] = _rms_rows(r[:, h0:h0 + HEAD_D], gain).astype(dtype)
        col += n


def _in_proj(x2, g, head_gains, w, outs, tm):
    t, d = x2.shape
    n_all = w.shape[1]
    est = 2 * tm * d * 4 + d * n_all * 2 + 2 * tm * n_all * 4
    return pl.pallas_call(
        functools.partial(_in_proj_kernel, outs),
        grid=(t // tm,),
        in_specs=[
            pl.BlockSpec((tm, d), lambda i: (i, 0)),
            pl.BlockSpec((1, d), lambda i: (0, 0)),
            pl.BlockSpec(head_gains.shape, lambda i: (0, 0)),
            pl.BlockSpec((d, n_all), lambda i: (0, 0), pipeline_mode=pl.Buffered(1)),
        ],
        out_specs=[pl.BlockSpec((tm, n), lambda i: (i, 0)) for n, _, _ in outs],
        out_shape=[jax.ShapeDtypeStruct((t, n), dtype) for n, dtype, _ in outs],
        compiler_params=pltpu.CompilerParams(
            dimension_semantics=("arbitrary",), vmem_limit_bytes=_vmem_limit(est)),
        name="in_proj",
    )(x2, g, head_gains, w)


def _dot3_pairs(a, b):
    return _dot(a[0], b[0]) + (_dot(a[0], b[1]) + _dot(a[1], b[0]))


def _tri_inverse_all(lowers, m16, m32):
    n = lowers[0].shape[0]
    eye = (lax.broadcasted_iota(jnp.int32, (n, n), 0)
           == lax.broadcasted_iota(jnp.int32, (n, n), 1)).astype(F32)
    not16_32 = m32 & jnp.logical_not(m16)
    ld = [_split(jnp.where(m16, l, 0.0)) for l in lowers]
    a = [_split(_dot3_pairs(x, x)) for x in ld]
    p = [eye - jnp.where(m16, l, 0.0) for l in lowers]
    for step in range(3):
        p = [pi + _dot3_pairs(_split(pi), ai) for pi, ai in zip(p, a)]
        if step < 2:
            a = [_split(_dot3_pairs(x, x)) for x in a]
    for sel in (lambda l: jnp.where(not16_32, l, 0.0), lambda l: jnp.where(m32, 0.0, l)):
        ps = [_split(pi) for pi in p]
        t = [_split(_dot3_pairs(_split(sel(l)), x)) for l, x in zip(lowers, ps)]
        p = [pi - _dot3_pairs(x, ti) for pi, x, ti in zip(p, ps, t)]
    return p


def _gdn_kernel(ts, qkv_ref, z_ref, ab_ref, cw_ref, gp_ref, ng_ref, o_ref,
                xp_ref, q_s, k_s, v_s, g_s, b_s, state_ref):
    s_blk = pl.program_id(1)
    w3 = 3 * BR_W

    @pl.when(s_blk == 0)
    def _():
        xp_ref[0:8, :] = jnp.zeros((8, w3), F32)
        state_ref[...] = jnp.zeros_like(state_ref)

    xp_ref[8:8 + ts, :] = qkv_ref[...]
    cw = cw_ref[...]
    acc = cw[0:1, :] * xp_ref[pl.ds(8 - (CONV_W - 1), ts), :]
    for j in range(1, CONV_W):
        acc = acc + cw[j:j + 1, :] * xp_ref[pl.ds(8 - (CONV_W - 1) + j, ts), :]
    xp_ref[0:8, :] = xp_ref[ts:ts + 8, :]
    act = _silu(acc)

    ab = ab_ref[...]
    g_all = -jnp.exp(gp_ref[0:1, :]) * _softplus(ab + gp_ref[1:2, :])
    beta_all = jax.nn.sigmoid(ab)
    for h in range(N_HEAD):
        sl = slice(h * HEAD_D, (h + 1) * HEAD_D)
        q = act[:, h * HEAD_D:(h + 1) * HEAD_D]
        k = act[:, BR_W + h * HEAD_D:BR_W + (h + 1) * HEAD_D]
        q_s[:, sl] = q * lax.rsqrt(jnp.sum(q * q, axis=-1, keepdims=True) + EPS) * (HEAD_D ** -0.5)
        k_s[:, sl] = k * lax.rsqrt(jnp.sum(k * k, axis=-1, keepdims=True) + EPS)
        g_s[:, sl] = jnp.broadcast_to(g_all[:, h:h + 1], (ts, HEAD_D))
        b_s[:, sl] = jnp.broadcast_to(beta_all[:, N_HEAD + h:N_HEAD + h + 1], (ts, HEAD_D))
    v_s[...] = act[:, 2 * BR_W:3 * BR_W]

    row = lax.broadcasted_iota(jnp.int32, (CHUNK, CHUNK), 0)
    col = lax.broadcasted_iota(jnp.int32, (CHUNK, CHUNK), 1)
    causal = row >= col
    strict = row > col
    tri = causal.astype(BF16)
    m16 = (row // 16) == (col // 16)
    m32 = (row // 32) == (col // 32)
    ng = ng_ref[...]

    nc = ts // CHUNK
    probs = [(c, h) for c in range(nc) for h in range(N_HEAD)]

    def blk(ref, p):
        c, h = p
        return ref[c * CHUNK:(c + 1) * CHUNK, h * HEAD_D:(h + 1) * HEAD_D]

    gcb = [_dot_exact_lhs(tri, blk(g_s, p)) for p in probs]
    decay = [jnp.exp(_dot_exact_lhs(tri, jnp.where(strict, blk(g_s, p)[:, :CHUNK], 0.0))) for p in probs]
    kk, qk = [], []
    for p in probs:
        k = blk(k_s, p)
        k16 = k.astype(BF16)
        kk.append(_dot_nt((k * blk(b_s, p)).astype(BF16), k16))
        qk.append(_dot_nt(blk(q_s, p).astype(BF16), k16))
    lower = [jnp.where(strict, x * dc, 0.0) for x, dc in zip(kk, decay)]
    a_qk = [jnp.where(causal, x * dc, 0.0).astype(BF16) for x, dc in zip(qk, decay)]
    t_inv = [x.astype(BF16) for x in _tri_inverse_all(lower, m16, m32)]
    u, w_q = [], []
    for p, ti, gc in zip(probs, t_inv, gcb):
        beta = blk(b_s, p)
        eg = jnp.exp(gc)
        u.append(_dot(ti, (blk(v_s, p) * beta).astype(BF16)))
        w = _dot(ti, (blk(k_s, p) * beta * eg).astype(BF16))
        w_q.append(jnp.concatenate([w.astype(BF16), (blk(q_s, p) * eg).astype(BF16)], axis=0))

    state = [state_ref[h] for h in range(N_HEAD)]
    for c in range(nc):
        ids = [c * N_HEAD + h for h in range(N_HEAD)]
        st16 = [s.astype(BF16) for s in state]
        ws = [_dot(w_q[i], st16[h]) for h, i in enumerate(ids)]
        v_new = [(u[i] - x[:CHUNK]).astype(BF16) for i, x in zip(ids, ws)]
        upd = []
        for h, i in enumerate(ids):
            g_last = gcb[i][CHUNK - 1:CHUNK, :]
            k_dec = (blk(k_s, probs[i]) * jnp.exp(g_last - gcb[i])).astype(BF16)
            upd.append(state[h] * jnp.exp(g_last) + _dot_tn(k_dec, v_new[h]))
        for h, i in enumerate(ids):
            o = ws[h][CHUNK:] + _dot(a_qk[i], v_new[h])
            rows = slice(c * CHUNK, (c + 1) * CHUNK)
            sl = slice(h * HEAD_D, (h + 1) * HEAD_D)
            o_ref[rows, sl] = (_rms_rows(o, ng) * _silu(z_ref[rows, sl])).astype(o_ref.dtype)
        state = upd
    for h in range(N_HEAD):
        state_ref[h] = state[h]


def _gdn(gqkv, gz, gab, conv_w, gp, ng, b, s, ts):
    w3 = 3 * BR_W
    nsb = s // ts
    est = (2 * ts * (w3 + BR_W + LANE) * 4 + 2 * ts * BR_W * 2 + (ts + 16) * w3 * 4
           + 5 * ts * BR_W * 4 + N_HEAD * HEAD_D * HEAD_D * 4 + 6 * ts * w3 * 4)
    return pl.pallas_call(
        functools.partial(_gdn_kernel, ts),
        grid=(b, nsb),
        in_specs=[
            pl.BlockSpec((ts, w3), lambda bi, si: (bi * nsb + si, 0)),
            pl.BlockSpec((ts, BR_W), lambda bi, si: (bi * nsb + si, 0)),
            pl.BlockSpec((ts, LANE), lambda bi, si: (bi * nsb + si, 0)),
            pl.BlockSpec((CONV_W, w3), lambda bi, si: (0, 0)),
            pl.BlockSpec((8, LANE), lambda bi, si: (0, 0)),
            pl.BlockSpec((1, HEAD_D), lambda bi, si: (0, 0)),
        ],
        out_specs=pl.BlockSpec((ts, BR_W), lambda bi, si: (bi * nsb + si, 0)),
        out_shape=jax.ShapeDtypeStruct((b * s, BR_W), BF16),
        scratch_shapes=[
            pltpu.VMEM((ts + 16, w3), F32),
            pltpu.VMEM((ts, BR_W), F32),
            pltpu.VMEM((ts, BR_W), F32),
            pltpu.VMEM((ts, BR_W), F32),
            pltpu.VMEM((ts, BR_W), F32),
            pltpu.VMEM((ts, BR_W), F32),
            pltpu.VMEM((N_HEAD, HEAD_D, HEAD_D), F32),
        ],
        compiler_params=pltpu.CompilerParams(
            dimension_semantics=("arbitrary", "arbitrary"), vmem_limit_bytes=_vmem_limit(est)),
        name="gdn",
    )(gqkv, gz, gab, conv_w, gp, ng)


SB_BLK = 128


def _sb_kernel(nsub, q_ref, k_ref, v_ref, o_ref, acc_ref, c_ref):
    i = pl.program_id(1)
    scale = HEAD_D ** -0.5
    row = lax.broadcasted_iota(jnp.int32, (SB_BLK, SB_BLK), 0)
    col = lax.broadcasted_iota(jnp.int32, (SB_BLK, SB_BLK), 1)
    before = col < row
    row2 = lax.broadcasted_iota(jnp.int32, (SB_BLK, 2 * SB_BLK), 0)
    col2 = lax.broadcasted_iota(jnp.int32, (SB_BLK, 2 * SB_BLK), 1)
    sum_rhs = jnp.logical_or(row2 > col2, col2 >= SB_BLK).astype(BF16)
    probs = [(j, h) for j in range(nsub) for h in range(N_HEAD)]
    rows = lambda j: slice(j * SB_BLK, (j + 1) * SB_BLK)
    lanes = lambda h: slice(h * HEAD_D, (h + 1) * HEAD_D)

    def step(t, diagonal):
        k0s, zs = [], []
        for j, h in probs:
            kb = nsub * i + j - t
            k0 = pl.multiple_of(jnp.maximum(kb, 0) * SB_BLK, SB_BLK)
            z = _dot_nt(q_ref[rows(j), lanes(h)], k_ref[pl.ds(k0, SB_BLK), lanes(h)]) * scale
            if not diagonal:
                z = z + jnp.where(kb >= 0, 0.0, -jnp.inf)
            k0s.append(k0)
            zs.append(z)
        log_1mb = [-_softplus(z) for z in zs]
        if diagonal:
            log_1mb = [jnp.where(before, x, 0.0) for x in log_1mb]
        sums = [_dot_exact_rhs(x, sum_rhs) for x in log_1mb]
        cmax = None
        for (j, h), k0, z, l1, sm in zip(probs, k0s, zs, log_1mb, sums):
            between = sm[:, :SB_BLK]
            c_new = sm[:, SB_BLK:]
            if not diagonal:
                c = c_ref[rows(j), lanes(h)]
                between = between + c
                c_new = c_new + c
            att = jnp.exp(z + l1 + between)
            if diagonal:
                att = jnp.where(before, att, 0.0)
            pv = _dot(att.astype(BF16), v_ref[pl.ds(k0, SB_BLK), lanes(h)])
            if diagonal:
                acc_ref[rows(j), lanes(h)] = pv
            else:
                acc_ref[rows(j), lanes(h)] += pv
            c_ref[rows(j), lanes(h)] = c_new
            cmax = c_new if cmax is None else jnp.maximum(cmax, c_new)
        return jnp.max(cmax)

    cmax0 = step(0, True)

    def cond(carry):
        t, cmax = carry
        return jnp.logical_and(t <= nsub * i + (nsub - 1), cmax > EXP_ZERO_BELOW)

    def body(carry):
        t, _ = carry
        return t + 1, step(t, False)

    lax.while_loop(cond, body, (jnp.int32(1), cmax0))
    o_ref[...] = acc_ref[...].astype(o_ref.dtype)


def _sb_attn(sq3, sk3, sv3, nsub):
    b, s, _ = sq3.shape
    tq = nsub * SB_BLK
    est = 4 * tq * BR_W * 2 + 2 * s * BR_W * 2 + 2 * tq * BR_W * 4 + 16 * nsub * N_HEAD * SB_BLK * SB_BLK * 4
    kv_spec = pl.BlockSpec((None, s, BR_W), lambda bi, i: (bi, 0, 0), pipeline_mode=pl.Buffered(1))
    return pl.pallas_call(
        functools.partial(_sb_kernel, nsub),
        grid=(b, s // tq),
        in_specs=[pl.BlockSpec((None, tq, BR_W), lambda bi, i: (bi, i, 0)), kv_spec, kv_spec],
        out_specs=pl.BlockSpec((None, tq, BR_W), lambda bi, i: (bi, i, 0)),
        out_shape=jax.ShapeDtypeStruct((b, s, BR_W), BF16),
        scratch_shapes=[pltpu.VMEM((tq, BR_W), F32), pltpu.VMEM((tq, BR_W), F32)],
        compiler_params=pltpu.CompilerParams(
            dimension_semantics=("arbitrary", "arbitrary"), vmem_limit_bytes=_vmem_limit(est)),
        name="sb_attn",
    )(sq3, sk3, sv3)


def _mem_kv_kernel(mem_ref, g_ref, w_ref, gk_ref, k_ref, v_ref):
    hn = _rms_rows(mem_ref[...], g_ref[...]).astype(BF16)
    kv = _dot(hn, w_ref[...])
    gk = gk_ref[...]
    for h in range(N_HEAD):
        sl = slice(h * HEAD_D, (h + 1) * HEAD_D)
        k_ref[:, sl] = _rms_rows(kv[:, sl], gk).astype(k_ref.dtype)
    v_ref[...] = kv[:, BR_W:].astype(v_ref.dtype)


def _mem_kv(mem, g, w, gk):
    b, m, d = mem.shape
    est = 2 * m * d * 4 + 2 * d * 2 * BR_W * 2 + 4 * m * BR_W * 2 + 4 * m * 2 * BR_W * 4
    return pl.pallas_call(
        _mem_kv_kernel,
        grid=(b,),
        in_specs=[
            pl.BlockSpec((None, m, d), lambda bi: (bi, 0, 0)),
            pl.BlockSpec((1, d), lambda bi: (0, 0)),
            pl.BlockSpec((d, 2 * BR_W), lambda bi: (0, 0)),
            pl.BlockSpec((1, HEAD_D), lambda bi: (0, 0)),
        ],
        out_specs=[pl.BlockSpec((None, m, BR_W), lambda bi: (bi, 0, 0))] * 2,
        out_shape=[jax.ShapeDtypeStruct((b, m, BR_W), BF16)] * 2,
        compiler_params=pltpu.CompilerParams(
            dimension_semantics=("arbitrary",), vmem_limit_bytes=_vmem_limit(est)),
        name="mem_kv",
    )(mem, g, w, gk)


def _mem_attn_kernel(q_ref, k_ref, v_ref, gq_ref, o_ref):
    scale = HEAD_D ** -0.5
    gq = gq_ref[...]
    for h in range(N_HEAD):
        sl = slice(h * HEAD_D, (h + 1) * HEAD_D)
        qn = _rms_rows(q_ref[:, sl], gq).astype(BF16)
        sc = _dot_nt(qn, k_ref[:, sl]) * scale
        e = jnp.exp(sc - jnp.max(sc, axis=-1, keepdims=True))
        o = _dot(e.astype(BF16), v_ref[:, sl])
        o_ref[:, sl] = (o / jnp.sum(e, axis=-1, keepdims=True)).astype(o_ref.dtype)


def _mem_attn(mq3, km, vm, gq, ts):
    b, s, _ = mq3.shape
    m = km.shape[1]
    est = 2 * ts * BR_W * 4 + 4 * m * BR_W * 2 + 2 * ts * BR_W * 2 + 6 * ts * m * 4
    return pl.pallas_call(
        _mem_attn_kernel,
        grid=(b, s // ts),
        in_specs=[
            pl.BlockSpec((None, ts, BR_W), lambda bi, si: (bi, si, 0)),
            pl.BlockSpec((None, m, BR_W), lambda bi, si: (bi, 0, 0)),
            pl.BlockSpec((None, m, BR_W), lambda bi, si: (bi, 0, 0)),
            pl.BlockSpec((1, HEAD_D), lambda bi, si: (0, 0)),
        ],
        out_specs=pl.BlockSpec((None, ts, BR_W), lambda bi, si: (bi, si, 0)),
        out_shape=jax.ShapeDtypeStruct((b, s, BR_W), BF16),
        compiler_params=pltpu.CompilerParams(
            dimension_semantics=("arbitrary", "arbitrary"), vmem_limit_bytes=_vmem_limit(est)),
        name="mem_attn",
    )(mq3, km, vm, gq)


def _merge_kernel(x_ref, gate_ref, og_ref, os_ref, om_ref, wg_ref, ws_ref, wm_ref, wo_ref, o_ref):
    d = x_ref.shape[1]
    mix = None
    for j, (b_ref, w_ref) in enumerate(((og_ref, wg_ref), (os_ref, ws_ref), (om_ref, wm_ref))):
        y = _dot(b_ref[...], w_ref[...])
        term = jax.nn.sigmoid(gate_ref[:, j * d:(j + 1) * d]) * y
        mix = term if mix is None else mix + term
    o_ref[...] = x_ref[...] + _dot(mix.astype(BF16), wo_ref[...])


def _merge(x2, gates, og, osb, om, wg, ws, wm, wo, tm):
    t, d = x2.shape
    est = (2 * tm * d * 4 + 2 * tm * 3 * d * 4 + 6 * tm * BR_W * 2 + 3 * BR_W * d * 2 + d * d * 2
           + 2 * tm * d * 4 + 4 * tm * d * 4)
    wspec = lambda shape: pl.BlockSpec(shape, lambda i: (0, 0), pipeline_mode=pl.Buffered(1))
    return pl.pallas_call(
        _merge_kernel,
        grid=(t // tm,),
        in_specs=[
            pl.BlockSpec((tm, d), lambda i: (i, 0)),
            pl.BlockSpec((tm, 3 * d), lambda i: (i, 0)),
            pl.BlockSpec((tm, BR_W), lambda i: (i, 0)),
            pl.BlockSpec((tm, BR_W), lambda i: (i, 0)),
            pl.BlockSpec((tm, BR_W), lambda i: (i, 0)),
            wspec((BR_W, d)), wspec((BR_W, d)), wspec((BR_W, d)), wspec((d, d)),
        ],
        out_specs=pl.BlockSpec((tm, d), lambda i: (i, 0)),
        out_shape=jax.ShapeDtypeStruct((t, d), F32),
        compiler_params=pltpu.CompilerParams(
            dimension_semantics=("arbitrary",), vmem_limit_bytes=_vmem_limit(est)),
        name="merge",
    )(x2, gates, og, osb, om, wg, ws, wm, wo)


FF_CHUNK = 1024


def _mlp_kernel(x_ref, g_ref, wu_ref, wd_ref, o_ref):
    x = x_ref[...]
    hn = _rms_rows(x, g_ref[...]).astype(BF16)
    acc = x
    for c0 in range(0, wu_ref.shape[1], FF_CHUNK):
        up = _dot(hn, wu_ref[:, c0:c0 + FF_CHUNK])
        act = jnp.square(jnp.maximum(up, 0.0)).astype(BF16)
        acc = acc + _dot(act, wd_ref[c0:c0 + FF_CHUNK, :])
    o_ref[...] = acc


def _mlp(x2, g, wu, wd, tm):
    t, d = x2.shape
    ff = wu.shape[1]
    est = 4 * tm * d * 4 + 2 * d * ff * 2 + 3 * tm * FF_CHUNK * 4 + 2 * tm * d * 4
    wspec = lambda shape: pl.BlockSpec(shape, lambda i: (0, 0), pipeline_mode=pl.Buffered(1))
    return pl.pallas_call(
        _mlp_kernel,
        grid=(t // tm,),
        in_specs=[
            pl.BlockSpec((tm, d), lambda i: (i, 0)),
            pl.BlockSpec((1, d), lambda i: (0, 0)),
            wspec((d, ff)), wspec((ff, d)),
        ],
        out_specs=pl.BlockSpec((tm, d), lambda i: (i, 0)),
        out_shape=jax.ShapeDtypeStruct((t, d), F32),
        compiler_params=pltpu.CompilerParams(
            dimension_semantics=("arbitrary",), vmem_limit_bytes=_vmem_limit(est)),
        name="mlp",
    )(x2, g, wu, wd)


def _pick(n, pref):
    t = min(n, pref)
    while n % t:
        t //= 2
    return t


def _layer(x, mem, norm1_g, w_in, conv_w, a_log, dt_bias, gdn_norm_g, sb_q_norm_g, sb_k_norm_g,
           mem_norm_g, w_mem_kv, mem_q_norm_g, mem_k_norm_g, w_br_gdn, w_br_sb, w_br_mem, w_o,
           norm2_g, w_up, w_down):
    b, s, d = x.shape
    t = b * s
    x2 = x.reshape(t, d)
    row = lambda v: v.reshape(1, -1).astype(F32)

    c = [0, 3 * BR_W, 4 * BR_W, 4 * BR_W + N_HEAD, 4 * BR_W + 2 * N_HEAD,
         7 * BR_W + 2 * N_HEAD, 8 * BR_W + 2 * N_HEAD, 8 * BR_W + 2 * N_HEAD + 3 * d]
    w_ab = jnp.pad(w_in[:, c[2]:c[4]], ((0, 0), (0, LANE - 2 * N_HEAD)))
    w_all = jnp.concatenate(
        [w_in[:, c[6]:c[7]], w_in[:, c[0]:c[1]], w_in[:, c[4]:c[5]], w_in[:, c[1]:c[2]],
         w_in[:, c[5]:c[6]], w_ab], axis=1).astype(BF16)
    outs = ((3 * d, F32, None), (3 * BR_W, F32, None), (BR_W, BF16, 0), (BR_W, BF16, 1), (BR_W, BF16, None),
            (BR_W, F32, None), (BR_W, F32, None), (LANE, F32, None))
    head_gains = jnp.zeros((8, HEAD_D), F32).at[0].set(sb_q_norm_g.astype(F32)).at[1].set(sb_k_norm_g.astype(F32))
    gates, gqkv, sq, sk, sv, gz, mq, gab = _in_proj(x2, row(norm1_g), head_gains, w_all, outs, _pick(t, 256))

    gp = jnp.zeros((8, LANE), F32)
    gp = gp.at[0, :N_HEAD].set(a_log.astype(F32)).at[1, :N_HEAD].set(dt_bias.astype(F32))
    o_gdn = _gdn(gqkv, gz, gab, conv_w.astype(F32), gp, row(gdn_norm_g), b, s, _pick(s, 256))

    to3 = lambda a: a.reshape(b, s, BR_W)
    o_sb = _sb_attn(to3(sq), to3(sk), to3(sv), 2 if s % (2 * SB_BLK) == 0 else 1)

    km, vm = _mem_kv(mem, row(mem_norm_g), w_mem_kv.astype(BF16), row(mem_k_norm_g))
    o_mem = _mem_attn(mq.reshape(b, s, BR_W), km, vm, row(mem_q_norm_g), _pick(s, 512))

    x1 = _merge(x2, gates, o_gdn, o_sb.reshape(t, BR_W), o_mem.reshape(t, BR_W),
                w_br_gdn.astype(BF16), w_br_sb.astype(BF16), w_br_mem.astype(BF16),
                w_o.astype(BF16), _pick(t, 512))
    out = _mlp(x1, row(norm2_g), w_up.astype(BF16), w_down.astype(BF16), _pick(t, 512))
    return out.reshape(b, s, d)


def kernel(x, mem, norm1_g, w_in, conv_w, a_log, dt_bias, gdn_norm_g, sb_q_norm_g, sb_k_norm_g,
           mem_norm_g, w_mem_kv, mem_q_norm_g, mem_k_norm_g, w_br_gdn, w_br_sb, w_br_mem, w_o,
           norm2_g, w_up, w_down):
    depth = w_in.shape[0]
    for l in range(depth):
        x = _layer(x, mem, norm1_g[l], w_in[l], conv_w[l], a_log[l], dt_bias[l], gdn_norm_g[l],
                   sb_q_norm_g[l], sb_k_norm_g[l], mem_norm_g[l], w_mem_kv[l], mem_q_norm_g[l],
                   mem_k_norm_g[l], w_br_gdn[l], w_br_sb[l], w_br_mem[l], w_o[l], norm2_g[l],
                   w_up[l], w_down[l])
    return x
```

```python
import functools

import jax
import jax.numpy as jnp
from jax import lax
from jax.experimental import pallas as pl
from jax.experimental.pallas import tpu as pltpu

F32 = jnp.float32
BF16 = jnp.bfloat16

EPS = 1e-6
N_HEAD = 4
HEAD_D = 128
BR_W = N_HEAD * HEAD_D
CONV_W = 4
CHUNK = 64
LANE = 128
V7X_VMEM_BYTES = 64 * 1024 * 1024
EXP_ZERO_BELOW = -104.0


def _vmem_limit(estimate_bytes):
    return int(min(estimate_bytes * 1.25 + (4 << 20), V7X_VMEM_BYTES - (6 << 20)))


def _dot(a, b):
    return jnp.dot(a, b, preferred_element_type=F32)


def _dot_nt(a, b):
    return lax.dot_general(a, b, (((1,), (1,)), ((), ())), preferred_element_type=F32)


def _dot_tn(a, b):
    return lax.dot_general(a, b, (((0,), (0,)), ((), ())), preferred_element_type=F32)


def _split(a):
    hi = a.astype(BF16)
    lo = (a - hi.astype(F32)).astype(BF16)
    return hi, lo


def _dot_exact_lhs(m_bf16, b):
    hi, lo = _split(b)
    return _dot(m_bf16, hi) + _dot(m_bf16, lo)


def _dot_exact_rhs(a, m_bf16):
    hi, lo = _split(a)
    return _dot(hi, m_bf16) + _dot(lo, m_bf16)


def _silu(x):
    return (0.5 * x) * (1.0 + jnp.tanh(0.5 * x))


def _softplus(x):
    return jnp.maximum(x, 0.0) + jnp.log1p(jnp.exp(-jnp.abs(x)))


def _rms_rows(x, g_row):
    ms = jnp.mean(x * x, axis=-1, keepdims=True)
    return x * lax.rsqrt(ms + EPS) * g_row


IN_COL_CHUNK = 512


def _sigmoid(x):
    return 0.5 + 0.5 * jnp.tanh(0.5 * x)


CONV_HALO = 8


def _in_proj_kernel(outs, tiles_per_seq, x_ref, g_ref, hg_ref, cw_ref, w_ref, *refs):
    out_refs, xp_ref = refs[:len(outs)], refs[len(outs)]
    tm = x_ref.shape[0]

    @pl.when(pl.program_id(0) % tiles_per_seq == 0)
    def _():
        xp_ref[0:CONV_HALO, :] = jnp.zeros((CONV_HALO, xp_ref.shape[1]), F32)

    hn = _rms_rows(x_ref[...], g_ref[...]).astype(BF16)
    col = 0
    for o_ref, (n, dtype, post) in zip(out_refs, outs):
        for c0 in range(0, n, IN_COL_CHUNK):
            cw = min(IN_COL_CHUNK, n - c0)
            cols = slice(c0, c0 + cw)
            r = _dot(hn, w_ref[:, col + c0:col + c0 + cw])
            if post is None:
                o_ref[:, cols] = r.astype(dtype)
            elif post == "sigmoid":
                o_ref[:, cols] = _sigmoid(r).astype(dtype)
            elif post == "gdn_qkv":
                xp_ref[CONV_HALO:CONV_HALO + tm, cols] = r
                acc = None
                for j in range(CONV_W):
                    term = cw_ref[j:j + 1, cols] * xp_ref[pl.ds(CONV_HALO - (CONV_W - 1) + j, tm), cols]
                    acc = term if acc is None else acc + term
                xp_ref[0:CONV_HALO, cols] = xp_ref[tm:tm + CONV_HALO, cols]
                act = _silu(acc)
                for h0 in range(0, cw, HEAD_D):
                    a = act[:, h0:h0 + HEAD_D]
                    if c0 + h0 < 2 * BR_W:
                        a = a * lax.rsqrt(jnp.sum(a * a, axis=-1, keepdims=True) + EPS)
                    if c0 + h0 < BR_W:
                        a = a * (HEAD_D ** -0.5)
                    o_ref[:, c0 + h0:c0 + h0 + HEAD_D] = a.astype(dtype)
            else:
                gain = hg_ref[post:post + 1, :]
                for h0 in range(0, cw, HEAD_D):
                    o_ref[:, c0 + h0:c0 + h0 + HEAD_D] = _rms_rows(r[:, h0:h0 + HEAD_D], gain).astype(dtype)
        col += n


def _in_proj(x2, g, head_gains, conv_w, w, outs, tm, seq):
    t, d = x2.shape
    n_all = w.shape[1]
    w3 = conv_w.shape[1]
    est = (2 * tm * d * 4 + d * n_all * 2 + 2 * tm * sum(n * jnp.dtype(dt).itemsize for n, dt, _ in outs)
           + (tm + CONV_HALO) * w3 * 4 + 8 * tm * IN_COL_CHUNK * 4)
    return pl.pallas_call(
        functools.partial(_in_proj_kernel, outs, seq // tm),
        grid=(t // tm,),
        in_specs=[
            pl.BlockSpec((tm, d), lambda i: (i, 0)),
            pl.BlockSpec((1, d), lambda i: (0, 0)),
            pl.BlockSpec(head_gains.shape, lambda i: (0, 0)),
            pl.BlockSpec(conv_w.shape, lambda i: (0, 0)),
            pl.BlockSpec((d, n_all), lambda i: (0, 0), pipeline_mode=pl.Buffered(1)),
        ],
        out_specs=[pl.BlockSpec((tm, n), lambda i: (i, 0)) for n, _, _ in outs],
        out_shape=[jax.ShapeDtypeStruct((t, n), dtype) for n, dtype, _ in outs],
        scratch_shapes=[pltpu.VMEM((tm + CONV_HALO, w3), F32)],
        compiler_params=pltpu.CompilerParams(
            dimension_semantics=("arbitrary",), vmem_limit_bytes=_vmem_limit(est)),
        name="in_proj",
    )(x2, g, head_gains, conv_w, w)


def _dot3_pairs(a, b):
    m = a[0].shape[0]
    both = _dot(jnp.concatenate([a[0], a[1]], axis=0), b[0])
    return both[:m] + (both[m:] + _dot(a[0], b[1]))


def _pair_masks():
    row = lax.broadcasted_iota(jnp.int32, (CHUNK, 2 * CHUNK), 0)
    lane = lax.broadcasted_iota(jnp.int32, (CHUNK, 2 * CHUNK), 1)
    col = lane & (CHUNK - 1)
    return dict(first=lane < CHUNK, strict=row > col, causal=row >= col, eye=row == col,
                m16=(row >> 4) == (col >> 4), m32=(row >> 5) == (col >> 5))


def _block_diag(x, first):
    zero = jnp.zeros_like(x)
    return jnp.concatenate([jnp.where(first, x, zero), jnp.where(first, zero, x)], axis=0)


def _tri_inverse_pairs(lowers, mk):
    right = lambda hl: (_block_diag(hl[0], mk["first"]), _block_diag(hl[1], mk["first"]))
    eye = mk["eye"].astype(F32)
    c32 = mk["m32"] & jnp.logical_not(mk["m16"])
    ld = [jnp.where(mk["m16"], l, 0.0) for l in lowers]
    p = [eye - x for x in ld]
    a_l = [_split(x) for x in ld]
    a_r = [right(x) for x in a_l]
    for _ in range(3):
        a_l = [_split(_dot3_pairs(xl, xr)) for xl, xr in zip(a_l, a_r)]
        a_r = [right(x) for x in a_l]
        p = [pi + _dot3_pairs(_split(pi), xr) for pi, xr in zip(p, a_r)]
    for sel in (lambda l: jnp.where(c32, l, 0.0), lambda l: jnp.where(mk["m32"], 0.0, l)):
        p16 = [pi.astype(BF16) for pi in p]
        p_r = [_block_diag(x, mk["first"]) for x in p16]
        t = [_block_diag(_dot(sel(l).astype(BF16), xr).astype(BF16), mk["first"]) for l, xr in zip(lowers, p_r)]
        p = [pi - _dot(xl, ti) for pi, xl, ti in zip(p, p16, t)]
    return p


def _gdn_kernel(ts, q_s, k_s, v_s, z_ref, ab_ref, gp_ref, ng_ref, o_ref, g_s, b_s, state_ref):
    @pl.when(pl.program_id(1) == 0)
    def _():
        state_ref[...] = jnp.zeros_like(state_ref)

    ab = ab_ref[...]
    g_all = -jnp.exp(gp_ref[0:1, :]) * _softplus(ab + gp_ref[1:2, :])
    beta_all = _sigmoid(ab)
    for h in range(N_HEAD):
        sl = slice(h * HEAD_D, (h + 1) * HEAD_D)
        g_s[:, sl] = jnp.broadcast_to(g_all[:, h:h + 1], (ts, HEAD_D))
        b_s[:, sl] = jnp.broadcast_to(beta_all[:, N_HEAD + h:N_HEAD + h + 1], (ts, HEAD_D))

    mk = _pair_masks()
    tri = (lax.broadcasted_iota(jnp.int32, (CHUNK, CHUNK), 0)
           >= lax.broadcasted_iota(jnp.int32, (CHUNK, CHUNK), 1)).astype(BF16)
    ng = ng_ref[...]
    zero_blk = jnp.zeros((CHUNK, HEAD_D), BF16)

    def blk(ref, c, h):
        return ref[c * CHUNK:(c + 1) * CHUNK, h * HEAD_D:(h + 1) * HEAD_D]

    def block_diag2(a, b):
        return jnp.concatenate([jnp.concatenate([a, zero_blk], axis=1),
                                jnp.concatenate([zero_blk, b], axis=1)], axis=0)

    nc = ts // CHUNK
    probs = [(c, h) for c in range(nc) for h in range(0, N_HEAD, 2)]

    gcb = [_dot_exact_lhs(tri, g_s[c * CHUNK:(c + 1) * CHUNK, h * HEAD_D:(h + 2) * HEAD_D]) for c, h in probs]
    decay = [jnp.exp(_dot_exact_lhs(tri, jnp.where(
        mk["strict"], jnp.where(mk["first"], blk(g_s, c, h), blk(g_s, c, h + 1)), 0.0))) for c, h in probs]
    kk, qk = [], []
    for c, h in probs:
        kk2, qk2 = [], []
        for hh in (h, h + 1):
            k = blk(k_s, c, hh)
            k16 = k.astype(BF16)
            kk2.append(_dot_nt((k * blk(b_s, c, hh)).astype(BF16), k16))
            qk2.append(_dot_nt(blk(q_s, c, hh).astype(BF16), k16))
        kk.append(jnp.concatenate(kk2, axis=1))
        qk.append(jnp.concatenate(qk2, axis=1))
    lower = [jnp.where(mk["strict"], x * dc, 0.0) for x, dc in zip(kk, decay)]
    a_qk = [jnp.where(mk["causal"], x * dc, 0.0).astype(BF16) for x, dc in zip(qk, decay)]
    t_inv = [x.astype(BF16) for x in _tri_inverse_pairs(lower, mk)]
    u, w_q = {}, {}
    for (c, h), ti, gc in zip(probs, t_inv, gcb):
        beta = [blk(b_s, c, hh) for hh in (h, h + 1)]
        eg = jnp.exp(gc)
        egs = (eg[:, :HEAD_D], eg[:, HEAD_D:])
        u2 = _dot(ti, block_diag2(*[(blk(v_s, c, hh) * bt).astype(BF16) for hh, bt in zip((h, h + 1), beta)]))
        w2 = _dot(ti, block_diag2(*[(blk(k_s, c, hh) * bt * e).astype(BF16)
                                    for hh, bt, e in zip((h, h + 1), beta, egs)]))
        for n, hh in enumerate((h, h + 1)):
            lanes = slice(n * HEAD_D, (n + 1) * HEAD_D)
            u[c, hh] = u2[:, lanes]
            w_q[c, hh] = jnp.concatenate([w2[:, lanes].astype(BF16), (blk(q_s, c, hh) * egs[n]).astype(BF16)], axis=0)

    state = [state_ref[h] for h in range(N_HEAD)]
    for c in range(nc):
        st16 = [s.astype(BF16) for s in state]
        ws = [_dot(w_q[c, h], st16[h]) for h in range(N_HEAD)]
        v_new = [(u[c, h] - ws[h][:CHUNK]).astype(BF16) for h in range(N_HEAD)]
        upd = []
        for h in range(N_HEAD):
            gc = gcb[c * (N_HEAD // 2) + h // 2][:, (h % 2) * HEAD_D:(h % 2 + 1) * HEAD_D]
            g_last = gc[CHUNK - 1:CHUNK, :]
            k_dec = (blk(k_s, c, h) * jnp.exp(g_last - gc)).astype(BF16)
            upd.append(state[h] * jnp.exp(g_last) + _dot_tn(k_dec, v_new[h]))
        for h in range(0, N_HEAD, 2):
            av = _dot(a_qk[c * (N_HEAD // 2) + h // 2], block_diag2(v_new[h], v_new[h + 1]))
            for n, hh in enumerate((h, h + 1)):
                o = ws[hh][CHUNK:] + av[:, n * HEAD_D:(n + 1) * HEAD_D]
                rows = slice(c * CHUNK, (c + 1) * CHUNK)
                sl = slice(hh * HEAD_D, (hh + 1) * HEAD_D)
                o_ref[rows, sl] = (_rms_rows(o, ng) * _silu(z_ref[rows, sl])).astype(o_ref.dtype)
        state = upd
    for h in range(N_HEAD):
        state_ref[h] = state[h]


def _gdn(gqkv, gz, gab, gp, ng, b, s, ts):
    nsb = s // ts
    est = (2 * ts * (4 * BR_W + LANE) * 4 + 2 * ts * BR_W * 2 + 2 * ts * BR_W * 4
           + N_HEAD * HEAD_D * HEAD_D * 4 + 24 * ts * BR_W * 4)
    tok = lambda n, j: pl.BlockSpec((ts, n), lambda bi, si: (bi * nsb + si, j))
    return pl.pallas_call(
        functools.partial(_gdn_kernel, ts),
        grid=(b, nsb),
        in_specs=[
            tok(BR_W, 0), tok(BR_W, 1), tok(BR_W, 2),
            tok(BR_W, 0), tok(LANE, 0),
            pl.BlockSpec((8, LANE), lambda bi, si: (0, 0)),
            pl.BlockSpec((1, HEAD_D), lambda bi, si: (0, 0)),
        ],
        out_specs=tok(BR_W, 0),
        out_shape=jax.ShapeDtypeStruct((b * s, BR_W), BF16),
        scratch_shapes=[
            pltpu.VMEM((ts, BR_W), F32),
            pltpu.VMEM((ts, BR_W), F32),
            pltpu.VMEM((N_HEAD, HEAD_D, HEAD_D), F32),
        ],
        compiler_params=pltpu.CompilerParams(
            dimension_semantics=("arbitrary", "arbitrary"), vmem_limit_bytes=_vmem_limit(est)),
        name="gdn",
    )(gqkv, gqkv, gqkv, gz, gab, gp, ng)


SB_BLK = 128


def _sb_kernel(nsub, q_ref, k_ref, v_ref, o_ref, acc_ref, c_ref):
    i = pl.program_id(1)
    scale = HEAD_D ** -0.5
    row = lax.broadcasted_iota(jnp.int32, (SB_BLK, SB_BLK), 0)
    col = lax.broadcasted_iota(jnp.int32, (SB_BLK, SB_BLK), 1)
    before = col < row
    row2 = lax.broadcasted_iota(jnp.int32, (SB_BLK, 2 * SB_BLK), 0)
    col2 = lax.broadcasted_iota(jnp.int32, (SB_BLK, 2 * SB_BLK), 1)
    sum_rhs = jnp.logical_or(row2 > col2, col2 >= SB_BLK).astype(BF16)
    probs = [(j, h) for j in range(nsub) for h in range(N_HEAD)]
    rows = lambda j: slice(j * SB_BLK, (j + 1) * SB_BLK)
    lanes = lambda h: slice(h * HEAD_D, (h + 1) * HEAD_D)

    def step(t, diagonal):
        k0s, zs = [], []
        for j, h in probs:
            kb = nsub * i + j - t
            k0 = pl.multiple_of(jnp.maximum(kb, 0) * SB_BLK, SB_BLK)
            z = _dot_nt(q_ref[rows(j), lanes(h)], k_ref[pl.ds(k0, SB_BLK), lanes(h)]) * scale
            if not diagonal:
                z = z + jnp.where(kb >= 0, 0.0, -jnp.inf)
            k0s.append(k0)
            zs.append(z)
        log_1mb = [-_softplus(z) for z in zs]
        if diagonal:
            log_1mb = [jnp.where(before, x, 0.0) for x in log_1mb]
        sums = [_dot_exact_rhs(x, sum_rhs) for x in log_1mb]
        cmax = None
        for (j, h), k0, z, l1, sm in zip(probs, k0s, zs, log_1mb, sums):
            between = sm[:, :SB_BLK]
            c_new = sm[:, SB_BLK:]
            if not diagonal:
                c = c_ref[rows(j), lanes(h)]
                between = between + c
                c_new = c_new + c
            att = jnp.exp(z + l1 + between)
            if diagonal:
                att = jnp.where(before, att, 0.0)
            pv = _dot(att.astype(BF16), v_ref[pl.ds(k0, SB_BLK), lanes(h)])
            if diagonal:
                acc_ref[rows(j), lanes(h)] = pv
            else:
                acc_ref[rows(j), lanes(h)] += pv
            c_ref[rows(j), lanes(h)] = c_new
            cmax = c_new if cmax is None else jnp.maximum(cmax, c_new)
        return jnp.max(cmax)

    cmax0 = step(0, True)

    def cond(carry):
        t, cmax = carry
        return jnp.logical_and(t <= nsub * i + (nsub - 1), cmax > EXP_ZERO_BELOW)

    def body(carry):
        t, _ = carry
        return t + 1, step(t, False)

    lax.while_loop(cond, body, (jnp.int32(1), cmax0))
    o_ref[...] = acc_ref[...].astype(o_ref.dtype)


def _sb_attn(sq3, sk3, sv3, nsub):
    b, s, _ = sq3.shape
    tq = nsub * SB_BLK
    est = 4 * tq * BR_W * 2 + 2 * s * BR_W * 2 + 2 * tq * BR_W * 4 + 16 * nsub * N_HEAD * SB_BLK * SB_BLK * 4
    kv_spec = pl.BlockSpec((None, s, BR_W), lambda bi, i: (bi, 0, 0), pipeline_mode=pl.Buffered(1))
    return pl.pallas_call(
        functools.partial(_sb_kernel, nsub),
        grid=(b, s // tq),
        in_specs=[pl.BlockSpec((None, tq, BR_W), lambda bi, i: (bi, i, 0)), kv_spec, kv_spec],
        out_specs=pl.BlockSpec((None, tq, BR_W), lambda bi, i: (bi, i, 0)),
        out_shape=jax.ShapeDtypeStruct((b, s, BR_W), BF16),
        scratch_shapes=[pltpu.VMEM((tq, BR_W), F32), pltpu.VMEM((tq, BR_W), F32)],
        compiler_params=pltpu.CompilerParams(
            dimension_semantics=("arbitrary", "arbitrary"), vmem_limit_bytes=_vmem_limit(est)),
        name="sb_attn",
    )(sq3, sk3, sv3)


def _mem_kv_kernel(mem_ref, g_ref, w_ref, gk_ref, k_ref, v_ref):
    hn = _rms_rows(mem_ref[...], g_ref[...]).astype(BF16)
    kv = _dot(hn, w_ref[...])
    gk = gk_ref[...]
    for h in range(N_HEAD):
        sl = slice(h * HEAD_D, (h + 1) * HEAD_D)
        k_ref[:, sl] = _rms_rows(kv[:, sl], gk).astype(k_ref.dtype)
    v_ref[...] = kv[:, BR_W:].astype(v_ref.dtype)


def _mem_kv(mem, g, w, gk):
    b, m, d = mem.shape
    est = 2 * m * d * 4 + 2 * d * 2 * BR_W * 2 + 4 * m * BR_W * 2 + 4 * m * 2 * BR_W * 4
    return pl.pallas_call(
        _mem_kv_kernel,
        grid=(b,),
        in_specs=[
            pl.BlockSpec((None, m, d), lambda bi: (bi, 0, 0)),
            pl.BlockSpec((1, d), lambda bi: (0, 0)),
            pl.BlockSpec((d, 2 * BR_W), lambda bi: (0, 0)),
            pl.BlockSpec((1, HEAD_D), lambda bi: (0, 0)),
        ],
        out_specs=[pl.BlockSpec((None, m, BR_W), lambda bi: (bi, 0, 0))] * 2,
        out_shape=[jax.ShapeDtypeStruct((b, m, BR_W), BF16)] * 2,
        compiler_params=pltpu.CompilerParams(
            dimension_semantics=("arbitrary",), vmem_limit_bytes=_vmem_limit(est)),
        name="mem_kv",
    )(mem, g, w, gk)


def _mem_attn_kernel(q_ref, k_ref, v_ref, gq_ref, o_ref):
    scale = HEAD_D ** -0.5
    gq = gq_ref[...]
    for h in range(N_HEAD):
        sl = slice(h * HEAD_D, (h + 1) * HEAD_D)
        qn = _rms_rows(q_ref[:, sl], gq).astype(BF16)
        sc = _dot_nt(qn, k_ref[:, sl]) * scale
        e = jnp.exp(sc - jnp.max(sc, axis=-1, keepdims=True))
        o = _dot(e.astype(BF16), v_ref[:, sl])
        o_ref[:, sl] = (o / jnp.sum(e, axis=-1, keepdims=True)).astype(o_ref.dtype)


def _mem_attn(mq3, km, vm, gq, ts):
    b, s, _ = mq3.shape
    m = km.shape[1]
    est = 2 * ts * BR_W * 4 + 4 * m * BR_W * 2 + 2 * ts * BR_W * 2 + 6 * ts * m * 4
    return pl.pallas_call(
        _mem_attn_kernel,
        grid=(b, s // ts),
        in_specs=[
            pl.BlockSpec((None, ts, BR_W), lambda bi, si: (bi, si, 0)),
            pl.BlockSpec((None, m, BR_W), lambda bi, si: (bi, 0, 0)),
            pl.BlockSpec((None, m, BR_W), lambda bi, si: (bi, 0, 0)),
            pl.BlockSpec((1, HEAD_D), lambda bi, si: (0, 0)),
        ],
        out_specs=pl.BlockSpec((None, ts, BR_W), lambda bi, si: (bi, si, 0)),
        out_shape=jax.ShapeDtypeStruct((b, s, BR_W), BF16),
        compiler_params=pltpu.CompilerParams(
            dimension_semantics=("arbitrary", "arbitrary"), vmem_limit_bytes=_vmem_limit(est)),
        name="mem_attn",
    )(mq3, km, vm, gq)


FF_CHUNK = 1024


def _merge_mlp_kernel(x_ref, gate_ref, og_ref, os_ref, om_ref, wg_ref, ws_ref, wm_ref, wo_ref,
                      g2_ref, wu_ref, wd_ref, o_ref):
    d = x_ref.shape[1]
    mix = None
    for j, (b_ref, w_ref) in enumerate(((og_ref, wg_ref), (os_ref, ws_ref), (om_ref, wm_ref))):
        term = gate_ref[:, j * d:(j + 1) * d].astype(F32) * _dot(b_ref[...], w_ref[...])
        mix = term if mix is None else mix + term
    x1 = x_ref[...] + _dot(mix.astype(BF16), wo_ref[...])
    hn = _rms_rows(x1, g2_ref[...]).astype(BF16)
    acc = x1
    for c0 in range(0, wu_ref.shape[1], FF_CHUNK):
        up = _dot(hn, wu_ref[:, c0:c0 + FF_CHUNK])
        act = jnp.square(jnp.maximum(up, 0.0)).astype(BF16)
        acc = acc + _dot(act, wd_ref[c0:c0 + FF_CHUNK, :])
    o_ref[...] = acc


def _merge_mlp(x2, gates, og, osb, om, wg, ws, wm, wo, g2, wu, wd, tm):
    t, d = x2.shape
    ff = wu.shape[1]
    est = (2 * tm * d * 4 + 2 * tm * 3 * d * 2 + 6 * tm * BR_W * 2 + (3 * BR_W * d + d * d + 2 * d * ff) * 2
           + 2 * tm * d * 4 + 6 * tm * d * 4 + 3 * tm * FF_CHUNK * 4)
    wspec = lambda shape: pl.BlockSpec(shape, lambda i: (0, 0), pipeline_mode=pl.Buffered(1))
    tile = lambda n: pl.BlockSpec((tm, n), lambda i: (i, 0))
    return pl.pallas_call(
        _merge_mlp_kernel,
        grid=(t // tm,),
        in_specs=[
            tile(d), tile(3 * d), tile(BR_W), tile(BR_W), tile(BR_W),
            wspec((BR_W, d)), wspec((BR_W, d)), wspec((BR_W, d)), wspec((d, d)),
            pl.BlockSpec((1, d), lambda i: (0, 0)), wspec((d, ff)), wspec((ff, d)),
        ],
        out_specs=tile(d),
        out_shape=jax.ShapeDtypeStruct((t, d), F32),
        compiler_params=pltpu.CompilerParams(
            dimension_semantics=("arbitrary",), vmem_limit_bytes=_vmem_limit(est)),
        name="merge_mlp",
    )(x2, gates, og, osb, om, wg, ws, wm, wo, g2, wu, wd)


def _pick(n, pref):
    t = min(n, pref)
    while n % t:
        t //= 2
    return t


def _layer(x, mem, norm1_g, w_in, conv_w, a_log, dt_bias, gdn_norm_g, sb_q_norm_g, sb_k_norm_g,
           mem_norm_g, w_mem_kv, mem_q_norm_g, mem_k_norm_g, w_br_gdn, w_br_sb, w_br_mem, w_o,
           norm2_g, w_up, w_down):
    b, s, d = x.shape
    t = b * s
    x2 = x.reshape(t, d)
    row = lambda v: v.reshape(1, -1).astype(F32)

    c = [0, 3 * BR_W, 4 * BR_W, 4 * BR_W + N_HEAD, 4 * BR_W + 2 * N_HEAD,
         7 * BR_W + 2 * N_HEAD, 8 * BR_W + 2 * N_HEAD, 8 * BR_W + 2 * N_HEAD + 3 * d]
    w_ab = jnp.pad(w_in[:, c[2]:c[4]], ((0, 0), (0, LANE - 2 * N_HEAD)))
    w_all = jnp.concatenate(
        [w_in[:, c[6]:c[7]], w_in[:, c[0]:c[1]], w_in[:, c[4]:c[5]], w_in[:, c[1]:c[2]],
         w_in[:, c[5]:c[6]], w_ab], axis=1).astype(BF16)
    outs = ((3 * d, BF16, "sigmoid"), (3 * BR_W, F32, "gdn_qkv"), (BR_W, BF16, 0), (BR_W, BF16, 1),
            (BR_W, BF16, None), (BR_W, F32, None), (BR_W, F32, None), (LANE, F32, None))
    head_gains = jnp.zeros((8, HEAD_D), F32).at[0].set(sb_q_norm_g.astype(F32)).at[1].set(sb_k_norm_g.astype(F32))
    gates, gqkv, sq, sk, sv, gz, mq, gab = _in_proj(
        x2, row(norm1_g), head_gains, conv_w.astype(F32), w_all, outs, _pick(s, 512), s)

    gp = jnp.zeros((8, LANE), F32)
    gp = gp.at[0, :N_HEAD].set(a_log.astype(F32)).at[1, :N_HEAD].set(dt_bias.astype(F32))
    o_gdn = _gdn(gqkv, gz, gab, gp, row(gdn_norm_g), b, s, _pick(s, 512))

    to3 = lambda a: a.reshape(b, s, BR_W)
    o_sb = _sb_attn(to3(sq), to3(sk), to3(sv), 2 if s % (2 * SB_BLK) == 0 else 1)

    km, vm = _mem_kv(mem, row(mem_norm_g), w_mem_kv.astype(BF16), row(mem_k_norm_g))
    o_mem = _mem_attn(mq.reshape(b, s, BR_W), km, vm, row(mem_q_norm_g), _pick(s, 512))

    out = _merge_mlp(x2, gates, o_gdn, o_sb.reshape(t, BR_W), o_mem.reshape(t, BR_W),
                     w_br_gdn.astype(BF16), w_br_sb.astype(BF16), w_br_mem.astype(BF16),
                     w_o.astype(BF16), row(norm2_g), w_up.astype(BF16), w_down.astype(BF16), _pick(t, 512))
    return out.reshape(b, s, d)


def kernel(x, mem, norm1_g, w_in, conv_w, a_log, dt_bias, gdn_norm_g, sb_q_norm_g, sb_k_norm_g,
           mem_norm_g, w_mem_kv, mem_q_norm_g, mem_k_norm_g, w_br_gdn, w_br_sb, w_br_mem, w_o,
           norm2_g, w_up, w_down):
    depth = w_in.shape[0]
    for l in range(depth):
        x = _layer(x, mem, norm1_g[l], w_in[l], conv_w[l], a_log[l], dt_bias[l], gdn_norm_g[l],
                   sb_q_norm_g[l], sb_k_norm_g[l], mem_norm_g[l], w_mem_kv[l], mem_q_norm_g[l],
                   mem_k_norm_g[l], w_br_gdn[l], w_br_sb[l], w_br_mem[l], w_o[l], norm2_g[l],
                   w_up[l], w_down[l])
    return x
```

```python
import functools

import jax
import jax.numpy as jnp
from jax import lax
from jax.experimental import pallas as pl
from jax.experimental.pallas import tpu as pltpu

F32 = jnp.float32
BF16 = jnp.bfloat16

EPS = 1e-6
N_HEAD = 4
HEAD_D = 128
BR_W = N_HEAD * HEAD_D
CONV_W = 4
CHUNK = 64
LANE = 128
V7X_VMEM_BYTES = 64 * 1024 * 1024
EXP_ZERO_BELOW = -104.0


def _vmem_limit(estimate_bytes):
    return int(min(estimate_bytes * 1.25 + (4 << 20), V7X_VMEM_BYTES - (6 << 20)))


def _dot(a, b):
    return jnp.dot(a, b, preferred_element_type=F32)


def _dot_nt(a, b):
    return lax.dot_general(a, b, (((1,), (1,)), ((), ())), preferred_element_type=F32)


def _dot_tn(a, b):
    return lax.dot_general(a, b, (((0,), (0,)), ((), ())), preferred_element_type=F32)


def _split(a):
    hi = a.astype(BF16)
    lo = (a - hi.astype(F32)).astype(BF16)
    return hi, lo


def _dot_exact_lhs(m_bf16, b):
    hi, lo = _split(b)
    return _dot(m_bf16, hi) + _dot(m_bf16, lo)


def _dot_exact_rhs(a, m_bf16):
    hi, lo = _split(a)
    return _dot(hi, m_bf16) + _dot(lo, m_bf16)


def _silu(x):
    return (0.5 * x) * (1.0 + jnp.tanh(0.5 * x))


def _softplus(x):
    return jnp.maximum(x, 0.0) + jnp.log1p(jnp.exp(-jnp.abs(x)))


def _rms_rows(x, g_row):
    ms = jnp.mean(x * x, axis=-1, keepdims=True)
    return x * lax.rsqrt(ms + EPS) * g_row


IN_COL_CHUNK = 512


def _sigmoid(x):
    return 0.5 + 0.5 * jnp.tanh(0.5 * x)


CONV_HALO = 8


def _in_proj_kernel(outs, n_w, tiles_per_seq, x_ref, g_ref, hg_ref, cw_ref, *refs):
    w_refs, out_refs, xp_ref = refs[:n_w], refs[n_w:n_w + len(outs)], refs[n_w + len(outs)]
    tm = x_ref.shape[0]

    @pl.when(pl.program_id(0) % tiles_per_seq == 0)
    def _():
        xp_ref[0:CONV_HALO, :] = jnp.zeros((CONV_HALO, xp_ref.shape[1]), F32)

    def conv_head(o_ref, h0):
        cols = slice(h0, h0 + HEAD_D)
        xe = xp_ref[:, cols]
        acc = None
        for j in range(CONV_W):
            shift = CONV_W - 1 - j
            rows = (pltpu.roll(xe, shift, axis=0) if shift else xe)[CONV_HALO:]
            term = cw_ref[j:j + 1, cols] * rows
            acc = term if acc is None else acc + term
        xp_ref[0:CONV_HALO, cols] = xe[tm:]
        a = _silu(acc)
        if h0 < 2 * BR_W:
            a = a * lax.rsqrt(jnp.sum(a * a, axis=-1, keepdims=True) + EPS)
        if h0 < BR_W:
            a = a * (HEAD_D ** -0.5)
        o_ref[:, cols] = a.astype(o_ref.dtype)

    hn = _rms_rows(x_ref[...], g_ref[...]).astype(BF16)
    for o_ref, (n, dtype, post, wi, wcol) in zip(out_refs, outs):
        for c0 in range(0, n, IN_COL_CHUNK):
            cw = min(IN_COL_CHUNK, n - c0)
            cols = slice(c0, c0 + cw)
            r = _dot(hn, w_refs[wi][:, wcol + c0:wcol + c0 + cw])
            if post == "gdn_qkv":
                xp_ref[CONV_HALO:CONV_HALO + tm, cols] = r
                for h0 in range(0, cw, HEAD_D):
                    conv_head(o_ref, c0 + h0)
                continue
            if post is None:
                o_ref[:, cols] = r.astype(dtype)
            elif post == "sigmoid":
                o_ref[:, cols] = _sigmoid(r).astype(dtype)
            else:
                gain = hg_ref[post:post + 1, :]
                for h0 in range(0, cw, HEAD_D):
                    o_ref[:, c0 + h0:c0 + h0 + HEAD_D] = _rms_rows(r[:, h0:h0 + HEAD_D], gain).astype(dtype)


def _in_proj(x2, g, head_gains, conv_w, ws, outs, tm, seq):
    t, d = x2.shape
    w3 = conv_w.shape[1]
    est = (2 * tm * d * 4 + sum(d * w.shape[1] * 2 for w in ws)
           + 2 * tm * sum(o[0] * jnp.dtype(o[1]).itemsize for o in outs)
           + (tm + CONV_HALO) * w3 * 4 + 8 * tm * IN_COL_CHUNK * 4)
    return pl.pallas_call(
        functools.partial(_in_proj_kernel, outs, len(ws), seq // tm),
        grid=(t // tm,),
        in_specs=[
            pl.BlockSpec((tm, d), lambda i: (i, 0)),
            pl.BlockSpec((1, d), lambda i: (0, 0)),
            pl.BlockSpec(head_gains.shape, lambda i: (0, 0)),
            pl.BlockSpec(conv_w.shape, lambda i: (0, 0)),
        ] + [pl.BlockSpec(w.shape, lambda i: (0, 0), pipeline_mode=pl.Buffered(1)) for w in ws],
        out_specs=[pl.BlockSpec((tm, o[0]), lambda i: (i, 0)) for o in outs],
        out_shape=[jax.ShapeDtypeStruct((t, o[0]), o[1]) for o in outs],
        scratch_shapes=[pltpu.VMEM((tm + CONV_HALO, w3), F32)],
        compiler_params=pltpu.CompilerParams(
            dimension_semantics=("arbitrary",), vmem_limit_bytes=_vmem_limit(est)),
        name="in_proj",
    )(x2, g, head_gains, conv_w, *ws)


def _dot3_pairs(a, b):
    m = a[0].shape[0]
    both = _dot(jnp.concatenate([a[0], a[1]], axis=0), b[0])
    return both[:m] + (both[m:] + _dot(a[0], b[1]))


def _pair_masks():
    row = lax.broadcasted_iota(jnp.int32, (CHUNK, 2 * CHUNK), 0)
    lane = lax.broadcasted_iota(jnp.int32, (CHUNK, 2 * CHUNK), 1)
    col = lane & (CHUNK - 1)
    return dict(first=lane < CHUNK, strict=row > col, causal=row >= col, eye=row == col,
                m16=(row >> 4) == (col >> 4), m32=(row >> 5) == (col >> 5))


def _block_diag(x, first):
    zero = jnp.zeros_like(x)
    return jnp.concatenate([jnp.where(first, x, zero), jnp.where(first, zero, x)], axis=0)


def _tri_inverse_pairs(lowers, mk):
    right = lambda hl: (_block_diag(hl[0], mk["first"]), _block_diag(hl[1], mk["first"]))
    eye = mk["eye"].astype(F32)
    c32 = mk["m32"] & jnp.logical_not(mk["m16"])
    ld = [jnp.where(mk["m16"], l, 0.0) for l in lowers]
    p = [eye - x for x in ld]
    a_l = [_split(x) for x in ld]
    a_r = [right(x) for x in a_l]
    for _ in range(3):
        a_l = [_split(_dot3_pairs(xl, xr)) for xl, xr in zip(a_l, a_r)]
        a_r = [right(x) for x in a_l]
        p = [pi + _dot3_pairs(_split(pi), xr) for pi, xr in zip(p, a_r)]
    for sel in (lambda l: jnp.where(c32, l, 0.0), lambda l: jnp.where(mk["m32"], 0.0, l)):
        p16 = [pi.astype(BF16) for pi in p]
        p_r = [_block_diag(x, mk["first"]) for x in p16]
        t = [_block_diag(_dot(sel(l).astype(BF16), xr).astype(BF16), mk["first"]) for l, xr in zip(lowers, p_r)]
        p = [pi - _dot(xl, ti) for pi, xl, ti in zip(p, p16, t)]
    return p


def _gdn_kernel(ts, q_s, k_s, v_s, z_ref, ab_ref, gp_ref, ng_ref, o_ref, g_s, b_s, state_ref):
    @pl.when(pl.program_id(1) == 0)
    def _():
        state_ref[...] = jnp.zeros_like(state_ref)

    ab = ab_ref[...]
    g_all = -jnp.exp(gp_ref[0:1, :]) * _softplus(ab + gp_ref[1:2, :])
    beta_all = _sigmoid(ab)
    for h in range(N_HEAD):
        sl = slice(h * HEAD_D, (h + 1) * HEAD_D)
        g_s[:, sl] = jnp.broadcast_to(g_all[:, h:h + 1], (ts, HEAD_D))
        b_s[:, sl] = jnp.broadcast_to(beta_all[:, N_HEAD + h:N_HEAD + h + 1], (ts, HEAD_D))

    mk = _pair_masks()
    tri = (lax.broadcasted_iota(jnp.int32, (CHUNK, CHUNK), 0)
           >= lax.broadcasted_iota(jnp.int32, (CHUNK, CHUNK), 1)).astype(BF16)
    ng = ng_ref[...]
    zero_blk = jnp.zeros((CHUNK, HEAD_D), BF16)

    def blk(ref, c, h):
        return ref[c * CHUNK:(c + 1) * CHUNK, h * HEAD_D:(h + 1) * HEAD_D]

    def block_diag2(a, b):
        return jnp.concatenate([jnp.concatenate([a, zero_blk], axis=1),
                                jnp.concatenate([zero_blk, b], axis=1)], axis=0)

    nc = ts // CHUNK
    probs = [(c, h) for c in range(nc) for h in range(0, N_HEAD, 2)]

    gcb = [_dot_exact_lhs(tri, g_s[c * CHUNK:(c + 1) * CHUNK, h * HEAD_D:(h + 2) * HEAD_D]) for c, h in probs]
    decay = [jnp.exp(_dot_exact_lhs(tri, jnp.where(
        mk["strict"], jnp.where(mk["first"], blk(g_s, c, h), blk(g_s, c, h + 1)), 0.0))) for c, h in probs]
    kk, qk = [], []
    for c, h in probs:
        kk2, qk2 = [], []
        for hh in (h, h + 1):
            k = blk(k_s, c, hh)
            k16 = k.astype(BF16)
            kk2.append(_dot_nt((k * blk(b_s, c, hh)).astype(BF16), k16))
            qk2.append(_dot_nt(blk(q_s, c, hh).astype(BF16), k16))
        kk.append(jnp.concatenate(kk2, axis=1))
        qk.append(jnp.concatenate(qk2, axis=1))
    lower = [jnp.where(mk["strict"], x * dc, 0.0) for x, dc in zip(kk, decay)]
    a_qk = [jnp.where(mk["causal"], x * dc, 0.0).astype(BF16) for x, dc in zip(qk, decay)]
    t_inv = [x.astype(BF16) for x in _tri_inverse_pairs(lower, mk)]
    wu, q_dec, k_dec, g_state = {}, {}, {}, {}
    for (c, h), ti, gc in zip(probs, t_inv, gcb):
        beta = [blk(b_s, c, hh) for hh in (h, h + 1)]
        eg = jnp.exp(gc)
        egs = (eg[:, :HEAD_D], eg[:, HEAD_D:])
        u2 = _dot(ti, block_diag2(*[(blk(v_s, c, hh) * bt).astype(BF16) for hh, bt in zip((h, h + 1), beta)]))
        w2 = _dot(ti, block_diag2(*[(blk(k_s, c, hh) * bt * e).astype(BF16)
                                    for hh, bt, e in zip((h, h + 1), beta, egs)]))
        for n, hh in enumerate((h, h + 1)):
            lanes = slice(n * HEAD_D, (n + 1) * HEAD_D)
            wu[c, hh] = jnp.concatenate([w2[:, lanes].astype(BF16), u2[:, lanes].astype(BF16)], axis=1)
            q_dec[c, hh] = blk(q_s, c, hh) * egs[n]
            g_last = gc[CHUNK - 1:CHUNK, lanes]
            k_dec[c, hh] = (blk(k_s, c, hh) * jnp.exp(g_last - gc[:, lanes])).astype(BF16)
            g_state[c, hh] = jnp.exp(g_last)
    k_wu = {p: _dot_tn(k_dec[p], wu[p]) for p in wu}
    zero_wu = jnp.zeros((CHUNK, 2 * HEAD_D), BF16)
    lhs, k_u, a_u = {}, {}, {}
    for (c, h), aq in zip(probs, a_qk):
        bd = jnp.concatenate([jnp.concatenate([wu[c, h], zero_wu], axis=1),
                              jnp.concatenate([zero_wu, wu[c, h + 1]], axis=1)], axis=0)
        a_wu = _dot(aq, bd)
        for n, hh in enumerate((h, h + 1)):
            a_w = a_wu[:, 2 * n * HEAD_D:(2 * n + 1) * HEAD_D]
            a_u[c, hh] = a_wu[:, (2 * n + 1) * HEAD_D:(2 * n + 2) * HEAD_D]
            k_u[c, hh] = k_wu[c, hh][:, HEAD_D:]
            lhs[c, hh] = jnp.concatenate([(-k_wu[c, hh][:, :HEAD_D]).astype(BF16),
                                          (q_dec[c, hh] - a_w).astype(BF16)], axis=0)

    state = [state_ref[h] for h in range(N_HEAD)]
    for c in range(nc):
        r = [_dot(lhs[c, h], state[h].astype(BF16)) for h in range(N_HEAD)]
        for h in range(N_HEAD):
            o = r[h][HEAD_D:] + a_u[c, h]
            rows = slice(c * CHUNK, (c + 1) * CHUNK)
            sl = slice(h * HEAD_D, (h + 1) * HEAD_D)
            o_ref[rows, sl] = (_rms_rows(o, ng) * _silu(z_ref[rows, sl])).astype(o_ref.dtype)
        state = [state[h] * g_state[c, h] + (k_u[c, h] + r[h][:HEAD_D]) for h in range(N_HEAD)]
    for h in range(N_HEAD):
        state_ref[h] = state[h]


def _gdn(gqkv, gz, gab, gp, ng, b, s, ts):
    nsb = s // ts
    est = (2 * ts * (4 * BR_W + LANE) * 4 + 2 * ts * BR_W * 2 + 2 * ts * BR_W * 4
           + N_HEAD * HEAD_D * HEAD_D * 4 + 24 * ts * BR_W * 4)
    tok = lambda n, j: pl.BlockSpec((ts, n), lambda bi, si: (bi * nsb + si, j))
    return pl.pallas_call(
        functools.partial(_gdn_kernel, ts),
        grid=(b, nsb),
        in_specs=[
            tok(BR_W, 0), tok(BR_W, 1), tok(BR_W, 2),
            tok(BR_W, 0), tok(LANE, 0),
            pl.BlockSpec((8, LANE), lambda bi, si: (0, 0)),
            pl.BlockSpec((1, HEAD_D), lambda bi, si: (0, 0)),
        ],
        out_specs=tok(BR_W, 0),
        out_shape=jax.ShapeDtypeStruct((b * s, BR_W), BF16),
        scratch_shapes=[
            pltpu.VMEM((ts, BR_W), F32),
            pltpu.VMEM((ts, BR_W), F32),
            pltpu.VMEM((N_HEAD, HEAD_D, HEAD_D), F32),
        ],
        compiler_params=pltpu.CompilerParams(
            dimension_semantics=("arbitrary", "arbitrary"), vmem_limit_bytes=_vmem_limit(est)),
        name="gdn",
    )(gqkv, gqkv, gqkv, gz, gab, gp, ng)


SB_BLK = 128


SB_NARROW = 32
LOG2E = 1.4426950408889634


def _sb_kernel(nsub, q_ref, k_ref, v_ref, o_ref, acc_ref, c_ref):
    i = pl.program_id(1)
    scale2 = (HEAD_D ** -0.5) * LOG2E
    dead_below = EXP_ZERO_BELOW * LOG2E
    row = lax.broadcasted_iota(jnp.int32, (SB_BLK, SB_BLK), 0)
    col = lax.broadcasted_iota(jnp.int32, (SB_BLK, SB_BLK), 1)
    before = col < row
    row2 = lax.broadcasted_iota(jnp.int32, (SB_BLK, 2 * SB_BLK), 0)
    col2 = lax.broadcasted_iota(jnp.int32, (SB_BLK, 2 * SB_BLK), 1)
    sum_rhs = jnp.logical_or(row2 > col2, col2 >= SB_BLK).astype(BF16)
    probs = [(j, h) for j in range(nsub) for h in range(N_HEAD)]
    lanes = lambda h: slice(h * HEAD_D, (h + 1) * HEAD_D)

    def step(t, diagonal, nrows):
        rows = lambda j: slice(j * SB_BLK, j * SB_BLK + nrows)
        k0s, zs = [], []
        for j, h in probs:
            kb = nsub * i + j - t
            k0 = pl.multiple_of(jnp.maximum(kb, 0) * SB_BLK, SB_BLK)
            z = _dot_nt(q_ref[rows(j), lanes(h)], k_ref[pl.ds(k0, SB_BLK), lanes(h)]) * scale2
            if not diagonal:
                z = z + jnp.where(kb >= 0, 0.0, -jnp.inf)
            k0s.append(k0)
            zs.append(z)
        log_1mb = [-(jnp.maximum(z, 0.0) + jnp.log2(1.0 + jnp.exp2(-jnp.abs(z)))) for z in zs]
        if diagonal:
            log_1mb = [jnp.where(before, x, 0.0) for x in log_1mb]
        sums = [_dot_exact_rhs(x, sum_rhs) for x in log_1mb]
        cmax = None
        for (j, h), k0, z, l1, sm in zip(probs, k0s, zs, log_1mb, sums):
            between = sm[:, :SB_BLK]
            c_new = sm[:, SB_BLK:]
            if not diagonal:
                c = c_ref[rows(j), lanes(h)]
                between = between + c
                c_new = c_new + c
            att = jnp.exp2(z + l1 + between)
            if diagonal:
                att = jnp.where(before, att, 0.0)
            pv = _dot(att.astype(BF16), v_ref[pl.ds(k0, SB_BLK), lanes(h)])
            if diagonal:
                acc_ref[rows(j), lanes(h)] = pv
            else:
                acc_ref[rows(j), lanes(h)] += pv
            c_ref[rows(j), lanes(h)] = c_new
            cmax = c_new if cmax is None else jnp.maximum(cmax, c_new)
        if nrows == SB_NARROW:
            return jnp.max(cmax), jnp.float32(-jnp.inf)
        return jnp.max(cmax[:SB_NARROW]), jnp.max(cmax[SB_NARROW:])

    def cond(carry):
        t, head, tail = carry
        return jnp.logical_and(t <= nsub * i + (nsub - 1), jnp.maximum(head, tail) > dead_below)

    def body(carry):
        t, _, tail = carry
        head, tail = lax.cond(tail > dead_below,
                              lambda: step(t, False, SB_BLK), lambda: step(t, False, SB_NARROW))
        return t + 1, head, tail

    lax.while_loop(cond, body, (jnp.int32(1),) + step(0, True, SB_BLK))
    o_ref[...] = acc_ref[...].astype(o_ref.dtype)


def _sb_attn(sq3, sk3, sv3, nsub):
    b, s, _ = sq3.shape
    tq = nsub * SB_BLK
    est = 4 * tq * BR_W * 2 + 2 * s * BR_W * 2 + 2 * tq * BR_W * 4 + 16 * nsub * N_HEAD * SB_BLK * SB_BLK * 4
    kv_spec = pl.BlockSpec((None, s, BR_W), lambda bi, i: (bi, 0, 0), pipeline_mode=pl.Buffered(1))
    return pl.pallas_call(
        functools.partial(_sb_kernel, nsub),
        grid=(b, s // tq),
        in_specs=[pl.BlockSpec((None, tq, BR_W), lambda bi, i: (bi, i, 0)), kv_spec, kv_spec],
        out_specs=pl.BlockSpec((None, tq, BR_W), lambda bi, i: (bi, i, 0)),
        out_shape=jax.ShapeDtypeStruct((b, s, BR_W), BF16),
        scratch_shapes=[pltpu.VMEM((tq, BR_W), F32), pltpu.VMEM((tq, BR_W), F32)],
        compiler_params=pltpu.CompilerParams(
            dimension_semantics=("arbitrary", "arbitrary"), vmem_limit_bytes=_vmem_limit(est)),
        name="sb_attn",
    )(sq3, sk3, sv3)


def _mem_kv_kernel(mem_ref, g_ref, w_ref, gk_ref, k_ref, v_ref):
    hn = _rms_rows(mem_ref[...], g_ref[...]).astype(BF16)
    kv = _dot(hn, w_ref[...])
    gk = gk_ref[...]
    for h in range(N_HEAD):
        sl = slice(h * HEAD_D, (h + 1) * HEAD_D)
        k_ref[:, sl] = _rms_rows(kv[:, sl], gk).astype(k_ref.dtype)
    v_ref[...] = kv[:, BR_W:].astype(v_ref.dtype)


def _mem_kv(mem, g, w, gk):
    b, m, d = mem.shape
    est = 2 * m * d * 4 + 2 * d * 2 * BR_W * 2 + 4 * m * BR_W * 2 + 4 * m * 2 * BR_W * 4
    return pl.pallas_call(
        _mem_kv_kernel,
        grid=(b,),
        in_specs=[
            pl.BlockSpec((None, m, d), lambda bi: (bi, 0, 0)),
            pl.BlockSpec((1, d), lambda bi: (0, 0)),
            pl.BlockSpec((d, 2 * BR_W), lambda bi: (0, 0)),
            pl.BlockSpec((1, HEAD_D), lambda bi: (0, 0)),
        ],
        out_specs=[pl.BlockSpec((None, m, BR_W), lambda bi: (bi, 0, 0))] * 2,
        out_shape=[jax.ShapeDtypeStruct((b, m, BR_W), BF16)] * 2,
        compiler_params=pltpu.CompilerParams(
            dimension_semantics=("arbitrary",), vmem_limit_bytes=_vmem_limit(est)),
        name="mem_kv",
    )(mem, g, w, gk)


def _mem_attn_kernel(q_ref, k_ref, v_ref, gq_ref, o_ref):
    scale = HEAD_D ** -0.5
    gq = gq_ref[...]
    for h in range(N_HEAD):
        sl = slice(h * HEAD_D, (h + 1) * HEAD_D)
        qn = _rms_rows(q_ref[:, sl], gq).astype(BF16)
        sc = _dot_nt(qn, k_ref[:, sl]) * scale
        e = jnp.exp(sc - jnp.max(sc, axis=-1, keepdims=True))
        o = _dot(e.astype(BF16), v_ref[:, sl])
        o_ref[:, sl] = (o / jnp.sum(e, axis=-1, keepdims=True)).astype(o_ref.dtype)


def _mem_attn(mq3, km, vm, gq, ts):
    b, s, _ = mq3.shape
    m = km.shape[1]
    est = 2 * ts * BR_W * 4 + 4 * m * BR_W * 2 + 2 * ts * BR_W * 2 + 6 * ts * m * 4
    return pl.pallas_call(
        _mem_attn_kernel,
        grid=(b, s // ts),
        in_specs=[
            pl.BlockSpec((None, ts, BR_W), lambda bi, si: (bi, si, 0)),
            pl.BlockSpec((None, m, BR_W), lambda bi, si: (bi, 0, 0)),
            pl.BlockSpec((None, m, BR_W), lambda bi, si: (bi, 0, 0)),
            pl.BlockSpec((1, HEAD_D), lambda bi, si: (0, 0)),
        ],
        out_specs=pl.BlockSpec((None, ts, BR_W), lambda bi, si: (bi, si, 0)),
        out_shape=jax.ShapeDtypeStruct((b, s, BR_W), BF16),
        compiler_params=pltpu.CompilerParams(
            dimension_semantics=("arbitrary", "arbitrary"), vmem_limit_bytes=_vmem_limit(est)),
        name="mem_attn",
    )(mq3, km, vm, gq)


FF_CHUNK = 1024


def _merge_mlp_kernel(x_ref, gate_ref, og_ref, os_ref, om_ref, wg_ref, ws_ref, wm_ref, wo_ref,
                      g2_ref, wu_ref, wd_ref, o_ref):
    d = x_ref.shape[1]
    mix = None
    for j, (b_ref, w_ref) in enumerate(((og_ref, wg_ref), (os_ref, ws_ref), (om_ref, wm_ref))):
        term = gate_ref[:, j * d:(j + 1) * d].astype(F32) * _dot(b_ref[...], w_ref[...])
        mix = term if mix is None else mix + term
    x1 = x_ref[...] + _dot(mix.astype(BF16), wo_ref[...])
    hn = _rms_rows(x1, g2_ref[...]).astype(BF16)
    acc = x1
    for c0 in range(0, wu_ref.shape[1], FF_CHUNK):
        up = _dot(hn, wu_ref[:, c0:c0 + FF_CHUNK])
        act = jnp.square(jnp.maximum(up, 0.0)).astype(BF16)
        acc = acc + _dot(act, wd_ref[c0:c0 + FF_CHUNK, :])
    o_ref[...] = acc


def _merge_mlp(x2, gates, og, osb, om, wg, ws, wm, wo, g2, wu, wd, tm):
    t, d = x2.shape
    ff = wu.shape[1]
    est = (2 * tm * d * 4 + 2 * tm * 3 * d * 2 + 6 * tm * BR_W * 2 + (3 * BR_W * d + d * d + 2 * d * ff) * 2
           + 2 * tm * d * 4 + 6 * tm * d * 4 + 3 * tm * FF_CHUNK * 4)
    wspec = lambda shape: pl.BlockSpec(shape, lambda i: (0, 0), pipeline_mode=pl.Buffered(1))
    tile = lambda n: pl.BlockSpec((tm, n), lambda i: (i, 0))
    return pl.pallas_call(
        _merge_mlp_kernel,
        grid=(t // tm,),
        in_specs=[
            tile(d), tile(3 * d), tile(BR_W), tile(BR_W), tile(BR_W),
            wspec((BR_W, d)), wspec((BR_W, d)), wspec((BR_W, d)), wspec((d, d)),
            pl.BlockSpec((1, d), lambda i: (0, 0)), wspec((d, ff)), wspec((ff, d)),
        ],
        out_specs=tile(d),
        out_shape=jax.ShapeDtypeStruct((t, d), F32),
        compiler_params=pltpu.CompilerParams(
            dimension_semantics=("arbitrary",), vmem_limit_bytes=_vmem_limit(est)),
        name="merge_mlp",
    )(x2, gates, og, osb, om, wg, ws, wm, wo, g2, wu, wd)


def _pick(n, pref):
    t = min(n, pref)
    while n % t:
        t //= 2
    return t


def _layer(x, mem, norm1_g, w_in, conv_w, a_log, dt_bias, gdn_norm_g, sb_q_norm_g, sb_k_norm_g,
           mem_norm_g, w_mem_kv, mem_q_norm_g, mem_k_norm_g, w_br_gdn, w_br_sb, w_br_mem, w_o,
           norm2_g, w_up, w_down):
    b, s, d = x.shape
    t = b * s
    x2 = x.reshape(t, d)
    row = lambda v: v.reshape(1, -1).astype(F32)

    n_ab = 2 * N_HEAD
    w_lo = w_in[:, :4 * BR_W].astype(BF16)
    w_ab = jnp.pad(w_in[:, 4 * BR_W:4 * BR_W + n_ab], ((0, 0), (0, LANE - n_ab))).astype(BF16)
    w_hi = w_in[:, 4 * BR_W + n_ab:].astype(BF16)
    outs = ((3 * d, BF16, "sigmoid", 2, 4 * BR_W), (3 * BR_W, F32, "gdn_qkv", 0, 0),
            (BR_W, BF16, 0, 2, 0), (BR_W, BF16, 1, 2, BR_W), (BR_W, BF16, None, 2, 2 * BR_W),
            (BR_W, F32, None, 0, 3 * BR_W), (BR_W, F32, None, 2, 3 * BR_W), (LANE, F32, None, 1, 0))
    head_gains = jnp.zeros((8, HEAD_D), F32).at[0].set(sb_q_norm_g.astype(F32)).at[1].set(sb_k_norm_g.astype(F32))
    gates, gqkv, sq, sk, sv, gz, mq, gab = _in_proj(
        x2, row(norm1_g), head_gains, conv_w.astype(F32), (w_lo, w_ab, w_hi), outs, _pick(s, 512), s)

    gp = jnp.zeros((8, LANE), F32)
    gp = gp.at[0, :N_HEAD].set(a_log.astype(F32)).at[1, :N_HEAD].set(dt_bias.astype(F32))
    o_gdn = _gdn(gqkv, gz, gab, gp, row(gdn_norm_g), b, s, _pick(s, 512))

    to3 = lambda a: a.reshape(b, s, BR_W)
    o_sb = _sb_attn(to3(sq), to3(sk), to3(sv), 2 if s % (2 * SB_BLK) == 0 else 1)

    km, vm = _mem_kv(mem, row(mem_norm_g), w_mem_kv.astype(BF16), row(mem_k_norm_g))
    o_mem = _mem_attn(mq.reshape(b, s, BR_W), km, vm, row(mem_q_norm_g), _pick(s, 512))

    out = _merge_mlp(x2, gates, o_gdn, o_sb.reshape(t, BR_W), o_mem.reshape(t, BR_W),
                     w_br_gdn.astype(BF16), w_br_sb.astype(BF16), w_br_mem.astype(BF16),
                     w_o.astype(BF16), row(norm2_g), w_up.astype(BF16), w_down.astype(BF16), _pick(t, 512))
    return out.reshape(b, s, d)


def kernel(x, mem, norm1_g, w_in, conv_w, a_log, dt_bias, gdn_norm_g, sb_q_norm_g, sb_k_norm_g,
           mem_norm_g, w_mem_kv, mem_q_norm_g, mem_k_norm_g, w_br_gdn, w_br_sb, w_br_mem, w_o,
           norm2_g, w_up, w_down):
    depth = w_in.shape[0]
    for l in range(depth):
        x = _layer(x, mem, norm1_g[l], w_in[l], conv_w[l], a_log[l], dt_bias[l], gdn_norm_g[l],
                   sb_q_norm_g[l], sb_k_norm_g[l], mem_norm_g[l], w_mem_kv[l], mem_q_norm_g[l],
                   mem_k_norm_g[l], w_br_gdn[l], w_br_sb[l], w_br_mem[l], w_o[l], norm2_g[l],
                   w_up[l], w_down[l])
    return x
```

```python
import functools

import jax
import jax.numpy as jnp
from jax import lax
from jax.experimental import pallas as pl
from jax.experimental.pallas import tpu as pltpu

F32 = jnp.float32
BF16 = jnp.bfloat16

EPS = 1e-6
N_HEAD = 4
HEAD_D = 128
BR_W = N_HEAD * HEAD_D
CONV_W = 4
CHUNK = 64
LANE = 128
V7X_VMEM_BYTES = 64 * 1024 * 1024
EXP_ZERO_BELOW = -104.0


def _vmem_limit(estimate_bytes):
    return int(min(estimate_bytes * 1.25 + (4 << 20), V7X_VMEM_BYTES - (6 << 20)))


def _dot(a, b):
    return jnp.dot(a, b, preferred_element_type=F32)


def _dot_nt(a, b):
    return lax.dot_general(a, b, (((1,), (1,)), ((), ())), preferred_element_type=F32)


def _dot_tn(a, b):
    return lax.dot_general(a, b, (((0,), (0,)), ((), ())), preferred_element_type=F32)


def _split(a):
    hi = a.astype(BF16)
    lo = (a - hi.astype(F32)).astype(BF16)
    return hi, lo


def _dot_exact_lhs(m_bf16, b):
    hi, lo = _split(b)
    return _dot(m_bf16, hi) + _dot(m_bf16, lo)


def _dot_exact_rhs(a, m_bf16):
    hi, lo = _split(a)
    return _dot(hi, m_bf16) + _dot(lo, m_bf16)


def _silu(x):
    return (0.5 * x) * (1.0 + jnp.tanh(0.5 * x))


def _sigmoid(x):
    return 0.5 + 0.5 * jnp.tanh(0.5 * x)


def _softplus(x):
    return jnp.maximum(x, 0.0) + jnp.log1p(jnp.exp(-jnp.abs(x)))


def _rms_rows(x, g_row):
    ms = jnp.mean(x * x, axis=-1, keepdims=True)
    return x * lax.rsqrt(ms + EPS) * g_row


IN_COL_CHUNK = 512
CONV_HALO = 8


def _in_proj_kernel(outs, n_w, tiles_per_seq, x_ref, g_ref, hg_ref, cw_ref, *refs):
    w_refs, out_refs, xp_ref = refs[:n_w], refs[n_w:n_w + len(outs)], refs[n_w + len(outs)]
    tm = x_ref.shape[0]

    @pl.when(pl.program_id(0) % tiles_per_seq == 0)
    def _():
        xp_ref[0:CONV_HALO, :] = jnp.zeros((CONV_HALO, xp_ref.shape[1]), F32)

    def conv_head(o_ref, h0):
        cols = slice(h0, h0 + HEAD_D)
        xe = xp_ref[:, cols]
        acc = None
        for j in range(CONV_W):
            shift = CONV_W - 1 - j
            rows = (pltpu.roll(xe, shift, axis=0) if shift else xe)[CONV_HALO:]
            term = cw_ref[j:j + 1, cols] * rows
            acc = term if acc is None else acc + term
        xp_ref[0:CONV_HALO, cols] = xe[tm:]
        a = _silu(acc)
        if h0 < 2 * BR_W:
            a = a * lax.rsqrt(jnp.sum(a * a, axis=-1, keepdims=True) + EPS)
        if h0 < BR_W:
            a = a * (HEAD_D ** -0.5)
        o_ref[:, cols] = a.astype(o_ref.dtype)

    hn = _rms_rows(x_ref[...], g_ref[...]).astype(BF16)
    for o_ref, (n, dtype, post, wi, wcol) in zip(out_refs, outs):
        for c0 in range(0, n, IN_COL_CHUNK):
            cw = min(IN_COL_CHUNK, n - c0)
            cols = slice(c0, c0 + cw)
            r = _dot(hn, w_refs[wi][:, wcol + c0:wcol + c0 + cw])
            if post == "gdn_qkv":
                xp_ref[CONV_HALO:CONV_HALO + tm, cols] = r
                for h0 in range(0, cw, HEAD_D):
                    conv_head(o_ref, c0 + h0)
                continue
            if post is None:
                o_ref[:, cols] = r.astype(dtype)
            elif post == "sigmoid":
                o_ref[:, cols] = _sigmoid(r).astype(dtype)
            else:
                gain = hg_ref[post:post + 1, :]
                for h0 in range(0, cw, HEAD_D):
                    o_ref[:, c0 + h0:c0 + h0 + HEAD_D] = _rms_rows(r[:, h0:h0 + HEAD_D], gain).astype(dtype)


def _in_proj(x2, g, head_gains, conv_w, ws, outs, tm, seq):
    t, d = x2.shape
    w3 = conv_w.shape[1]
    est = (2 * tm * d * 4 + sum(d * w.shape[1] * 2 for w in ws)
           + 2 * tm * sum(o[0] * jnp.dtype(o[1]).itemsize for o in outs)
           + (tm + CONV_HALO) * w3 * 4 + 8 * tm * IN_COL_CHUNK * 4)
    return pl.pallas_call(
        functools.partial(_in_proj_kernel, outs, len(ws), seq // tm),
        grid=(t // tm,),
        in_specs=[
            pl.BlockSpec((tm, d), lambda i: (i, 0)),
            pl.BlockSpec((1, d), lambda i: (0, 0)),
            pl.BlockSpec(head_gains.shape, lambda i: (0, 0)),
            pl.BlockSpec(conv_w.shape, lambda i: (0, 0)),
        ] + [pl.BlockSpec(w.shape, lambda i: (0, 0), pipeline_mode=pl.Buffered(1)) for w in ws],
        out_specs=[pl.BlockSpec((tm, o[0]), lambda i: (i, 0)) for o in outs],
        out_shape=[jax.ShapeDtypeStruct((t, o[0]), o[1]) for o in outs],
        scratch_shapes=[pltpu.VMEM((tm + CONV_HALO, w3), F32)],
        compiler_params=pltpu.CompilerParams(
            dimension_semantics=("arbitrary",), vmem_limit_bytes=_vmem_limit(est)),
        name="in_proj",
    )(x2, g, head_gains, conv_w, *ws)


def _dot3_pairs(a, b):
    m = a[0].shape[0]
    both = _dot(jnp.concatenate([a[0], a[1]], axis=0), b[0])
    return both[:m] + (both[m:] + _dot(a[0], b[1]))


def _pair_masks():
    row = lax.broadcasted_iota(jnp.int32, (CHUNK, 2 * CHUNK), 0)
    lane = lax.broadcasted_iota(jnp.int32, (CHUNK, 2 * CHUNK), 1)
    col = lane & (CHUNK - 1)
    return dict(first=lane < CHUNK, strict=row > col, causal=row >= col, eye=row == col,
                m8=(row >> 3) == (col >> 3), m16=(row >> 4) == (col >> 4), m32=(row >> 5) == (col >> 5))


def _block_diag(x, first):
    zero = jnp.zeros_like(x)
    return jnp.concatenate([jnp.where(first, x, zero), jnp.where(first, zero, x)], axis=0)


def _tri_inverse_pairs(lowers, mk):
    right = lambda hl: (_block_diag(hl[0], mk["first"]), _block_diag(hl[1], mk["first"]))
    eye = mk["eye"].astype(F32)
    c16 = mk["m16"] & jnp.logical_not(mk["m8"])
    c32 = mk["m32"] & jnp.logical_not(mk["m16"])
    ld = [jnp.where(mk["m8"], l, 0.0) for l in lowers]
    p = [eye - x for x in ld]
    a_l = [_split(x) for x in ld]
    a_r = [right(x) for x in a_l]
    for _ in range(2):
        a_l = [_split(_dot3_pairs(xl, xr)) for xl, xr in zip(a_l, a_r)]
        a_r = [right(x) for x in a_l]
        p = [pi + _dot3_pairs(_split(pi), xr) for pi, xr in zip(p, a_r)]
    for sel in (lambda l: jnp.where(c16, l, 0.0), lambda l: jnp.where(c32, l, 0.0),
                lambda l: jnp.where(mk["m32"], 0.0, l)):
        p16 = [pi.astype(BF16) for pi in p]
        p_r = [_block_diag(x, mk["first"]) for x in p16]
        t = [_block_diag(_dot(sel(l).astype(BF16), xr).astype(BF16), mk["first"]) for l, xr in zip(lowers, p_r)]
        p = [pi - _dot(xl, ti) for pi, xl, ti in zip(p, p16, t)]
    return p


def _gdn_kernel(ts, q_s, k_s, v_s, z_ref, ab_ref, gp_ref, ng_ref, o_ref, g_s, b_s, state_ref):
    @pl.when(pl.program_id(1) == 0)
    def _():
        state_ref[...] = jnp.zeros_like(state_ref)

    ab = ab_ref[...]
    g_all = -jnp.exp(gp_ref[0:1, :]) * _softplus(ab + gp_ref[1:2, :])
    beta_all = _sigmoid(ab)
    for h in range(N_HEAD):
        sl = slice(h * HEAD_D, (h + 1) * HEAD_D)
        g_s[:, sl] = jnp.broadcast_to(g_all[:, h:h + 1], (ts, HEAD_D))
        b_s[:, sl] = jnp.broadcast_to(beta_all[:, N_HEAD + h:N_HEAD + h + 1], (ts, HEAD_D))

    mk = _pair_masks()
    tri = (lax.broadcasted_iota(jnp.int32, (CHUNK, CHUNK), 0)
           >= lax.broadcasted_iota(jnp.int32, (CHUNK, CHUNK), 1)).astype(BF16)
    ng = ng_ref[...]
    zero_blk = jnp.zeros((CHUNK, HEAD_D), BF16)

    def blk(ref, c, h):
        return ref[c * CHUNK:(c + 1) * CHUNK, h * HEAD_D:(h + 1) * HEAD_D]

    def block_diag2(a, b):
        return jnp.concatenate([jnp.concatenate([a, zero_blk], axis=1),
                                jnp.concatenate([zero_blk, b], axis=1)], axis=0)

    nc = ts // CHUNK
    probs = [(c, h) for c in range(nc) for h in range(0, N_HEAD, 2)]

    gcb = [_dot_exact_lhs(tri, g_s[c * CHUNK:(c + 1) * CHUNK, h * HEAD_D:(h + 2) * HEAD_D]) for c, h in probs]
    decay = [jnp.exp(_dot_exact_lhs(tri, jnp.where(
        mk["strict"], jnp.where(mk["first"], blk(g_s, c, h), blk(g_s, c, h + 1)), 0.0))) for c, h in probs]
    kk, qk = [], []
    for c, h in probs:
        kk2, qk2 = [], []
        for hh in (h, h + 1):
            k = blk(k_s, c, hh)
            k16 = k.astype(BF16)
            kk2.append(_dot_nt((k * blk(b_s, c, hh)).astype(BF16), k16))
            qk2.append(_dot_nt(blk(q_s, c, hh).astype(BF16), k16))
        kk.append(jnp.concatenate(kk2, axis=1))
        qk.append(jnp.concatenate(qk2, axis=1))
    lower = [jnp.where(mk["strict"], x * dc, 0.0) for x, dc in zip(kk, decay)]
    a_qk = [jnp.where(mk["causal"], x * dc, 0.0).astype(BF16) for x, dc in zip(qk, decay)]
    t_inv = [x.astype(BF16) for x in _tri_inverse_pairs(lower, mk)]
    wu, q_dec, k_dec, g_state = {}, {}, {}, {}
    for (c, h), ti, gc in zip(probs, t_inv, gcb):
        beta = [blk(b_s, c, hh) for hh in (h, h + 1)]
        eg = jnp.exp(gc)
        egs = (eg[:, :HEAD_D], eg[:, HEAD_D:])
        u2 = _dot(ti, block_diag2(*[(blk(v_s, c, hh) * bt).astype(BF16) for hh, bt in zip((h, h + 1), beta)]))
        w2 = _dot(ti, block_diag2(*[(blk(k_s, c, hh) * bt * e).astype(BF16)
                                    for hh, bt, e in zip((h, h + 1), beta, egs)]))
        for n, hh in enumerate((h, h + 1)):
            lanes = slice(n * HEAD_D, (n + 1) * HEAD_D)
            wu[c, hh] = jnp.concatenate([w2[:, lanes].astype(BF16), u2[:, lanes].astype(BF16)], axis=1)
            q_dec[c, hh] = blk(q_s, c, hh) * egs[n]
            g_last = gc[CHUNK - 1:CHUNK, lanes]
            k_dec[c, hh] = (blk(k_s, c, hh) * jnp.exp(g_last - gc[:, lanes])).astype(BF16)
            g_state[c, hh] = jnp.exp(g_last)
    k_wu = {p: _dot_tn(k_dec[p], wu[p]) for p in wu}
    zero_wu = jnp.zeros((CHUNK, 2 * HEAD_D), BF16)
    lhs, k_u, a_u = {}, {}, {}
    for (c, h), aq in zip(probs, a_qk):
        bd = jnp.concatenate([jnp.concatenate([wu[c, h], zero_wu], axis=1),
                              jnp.concatenate([zero_wu, wu[c, h + 1]], axis=1)], axis=0)
        a_wu = _dot(aq, bd)
        for n, hh in enumerate((h, h + 1)):
            a_w = a_wu[:, 2 * n * HEAD_D:(2 * n + 1) * HEAD_D]
            a_u[c, hh] = a_wu[:, (2 * n + 1) * HEAD_D:(2 * n + 2) * HEAD_D]
            k_u[c, hh] = k_wu[c, hh][:, HEAD_D:]
            lhs[c, hh] = jnp.concatenate([(-k_wu[c, hh][:, :HEAD_D]).astype(BF16),
                                          (q_dec[c, hh] - a_w).astype(BF16)], axis=0)

    state = [state_ref[h] for h in range(N_HEAD)]
    for c in range(nc):
        r = [_dot(lhs[c, h], state[h].astype(BF16)) for h in range(N_HEAD)]
        for h in range(N_HEAD):
            o = r[h][HEAD_D:] + a_u[c, h]
            rows = slice(c * CHUNK, (c + 1) * CHUNK)
            sl = slice(h * HEAD_D, (h + 1) * HEAD_D)
            o_ref[rows, sl] = (_rms_rows(o, ng) * _silu(z_ref[rows, sl])).astype(o_ref.dtype)
        state = [state[h] * g_state[c, h] + (k_u[c, h] + r[h][:HEAD_D]) for h in range(N_HEAD)]
    for h in range(N_HEAD):
        state_ref[h] = state[h]


def _gdn(gqkv, gz, gab, gp, ng, b, s, ts):
    nsb = s // ts
    est = (2 * ts * (4 * BR_W + LANE) * 4 + 2 * ts * BR_W * 2 + 2 * ts * BR_W * 4
           + N_HEAD * HEAD_D * HEAD_D * 4 + 24 * ts * BR_W * 4)
    tok = lambda n, j: pl.BlockSpec((ts, n), lambda bi, si: (bi * nsb + si, j))
    return pl.pallas_call(
        functools.partial(_gdn_kernel, ts),
        grid=(b, nsb),
        in_specs=[
            tok(BR_W, 0), tok(BR_W, 1), tok(BR_W, 2),
            tok(BR_W, 0), tok(LANE, 0),
            pl.BlockSpec((8, LANE), lambda bi, si: (0, 0)),
            pl.BlockSpec((1, HEAD_D), lambda bi, si: (0, 0)),
        ],
        out_specs=tok(BR_W, 0),
        out_shape=jax.ShapeDtypeStruct((b * s, BR_W), BF16),
        scratch_shapes=[
            pltpu.VMEM((ts, BR_W), F32),
            pltpu.VMEM((ts, BR_W), F32),
            pltpu.VMEM((N_HEAD, HEAD_D, HEAD_D), F32),
        ],
        compiler_params=pltpu.CompilerParams(
            dimension_semantics=("arbitrary", "arbitrary"), vmem_limit_bytes=_vmem_limit(est)),
        name="gdn",
    )(gqkv, gqkv, gqkv, gz, gab, gp, ng)


SB_BLK = 128
SB_NARROW = 32
LOG2E = 1.4426950408889634


def _sb_kernel(nsub, q_ref, k_ref, v_ref, o_ref, acc_ref, c_ref):
    i = pl.program_id(1)
    scale2 = (HEAD_D ** -0.5) * LOG2E
    dead_below = EXP_ZERO_BELOW * LOG2E
    row = lax.broadcasted_iota(jnp.int32, (SB_BLK, SB_BLK), 0)
    col = lax.broadcasted_iota(jnp.int32, (SB_BLK, SB_BLK), 1)
    before = col < row
    row2 = lax.broadcasted_iota(jnp.int32, (SB_BLK, 2 * SB_BLK), 0)
    col2 = lax.broadcasted_iota(jnp.int32, (SB_BLK, 2 * SB_BLK), 1)
    sum_rhs = jnp.logical_or(row2 > col2, col2 >= SB_BLK).astype(BF16)
    probs = [(j, h) for j in range(nsub) for h in range(N_HEAD)]
    lanes = lambda h: slice(h * HEAD_D, (h + 1) * HEAD_D)

    def step(t, diagonal, nrows):
        rows = lambda j: slice(j * SB_BLK, j * SB_BLK + nrows)
        k0s, zs = [], []
        for j, h in probs:
            kb = nsub * i + j - t
            k0 = pl.multiple_of(jnp.maximum(kb, 0) * SB_BLK, SB_BLK)
            z = _dot_nt(q_ref[rows(j), lanes(h)], k_ref[pl.ds(k0, SB_BLK), lanes(h)]) * scale2
            if not diagonal:
                z = z + jnp.where(kb >= 0, 0.0, -jnp.inf)
            k0s.append(k0)
            zs.append(z)
        log_1mb = [-(jnp.maximum(z, 0.0) + jnp.log2(1.0 + jnp.exp2(-jnp.abs(z)))) for z in zs]
        if diagonal:
            log_1mb = [jnp.where(before, x, 0.0) for x in log_1mb]
        sums = [_dot_exact_rhs(x, sum_rhs) for x in log_1mb]
        cmax = None
        for (j, h), k0, z, l1, sm in zip(probs, k0s, zs, log_1mb, sums):
            between = sm[:, :SB_BLK]
            c_new = sm[:, SB_BLK:]
            if not diagonal:
                c = c_ref[rows(j), lanes(h)]
                between = between + c
                c_new = c_new + c
            att = jnp.exp2(z + l1 + between)
            if diagonal:
                att = jnp.where(before, att, 0.0)
            pv = _dot(att.astype(BF16), v_ref[pl.ds(k0, SB_BLK), lanes(h)])
            if diagonal:
                acc_ref[rows(j), lanes(h)] = pv
            else:
                acc_ref[rows(j), lanes(h)] += pv
            c_ref[rows(j), lanes(h)] = c_new
            cmax = c_new if cmax is None else jnp.maximum(cmax, c_new)
        if nrows == SB_NARROW:
            return jnp.max(cmax), jnp.float32(-jnp.inf)
        return jnp.max(cmax[:SB_NARROW]), jnp.max(cmax[SB_NARROW:])

    def cond(carry):
        t, head, tail = carry
        return jnp.logical_and(t <= nsub * i + (nsub - 1), jnp.maximum(head, tail) > dead_below)

    def body(carry):
        t, _, tail = carry
        head, tail = lax.cond(tail > dead_below,
                              lambda: step(t, False, SB_BLK), lambda: step(t, False, SB_NARROW))
        return t + 1, head, tail

    lax.while_loop(cond, body, (jnp.int32(1),) + step(0, True, SB_BLK))
    o_ref[...] = acc_ref[...].astype(o_ref.dtype)


def _sb_attn(sq3, sk3, sv3, nsub):
    b, s, _ = sq3.shape
    tq = nsub * SB_BLK
    est = 4 * tq * BR_W * 2 + 4 * s * BR_W * 2 + 2 * tq * BR_W * 4 + 16 * nsub * N_HEAD * SB_BLK * SB_BLK * 4
    kv_spec = pl.BlockSpec((None, s, BR_W), lambda bi, i: (bi, 0, 0))
    return pl.pallas_call(
        functools.partial(_sb_kernel, nsub),
        grid=(b, s // tq),
        in_specs=[pl.BlockSpec((None, tq, BR_W), lambda bi, i: (bi, i, 0)), kv_spec, kv_spec],
        out_specs=pl.BlockSpec((None, tq, BR_W), lambda bi, i: (bi, i, 0)),
        out_shape=jax.ShapeDtypeStruct((b, s, BR_W), BF16),
        scratch_shapes=[pltpu.VMEM((tq, BR_W), F32), pltpu.VMEM((tq, BR_W), F32)],
        compiler_params=pltpu.CompilerParams(
            dimension_semantics=("arbitrary", "arbitrary"), vmem_limit_bytes=_vmem_limit(est)),
        name="sb_attn",
    )(sq3, sk3, sv3)


def _mem_kv_kernel(mem_ref, g_ref, w_ref, gk_ref, k_ref, v_ref):
    hn = _rms_rows(mem_ref[...], g_ref[...]).astype(BF16)
    kv = _dot(hn, w_ref[...])
    gk = gk_ref[...]
    for h in range(N_HEAD):
        sl = slice(h * HEAD_D, (h + 1) * HEAD_D)
        k_ref[:, sl] = _rms_rows(kv[:, sl], gk).astype(k_ref.dtype)
    v_ref[...] = kv[:, BR_W:].astype(v_ref.dtype)


def _mem_kv(mem, g, w, gk):
    b, m, d = mem.shape
    est = 2 * m * d * 4 + 2 * d * 2 * BR_W * 2 + 4 * m * BR_W * 2 + 4 * m * 2 * BR_W * 4
    return pl.pallas_call(
        _mem_kv_kernel,
        grid=(b,),
        in_specs=[
            pl.BlockSpec((None, m, d), lambda bi: (bi, 0, 0)),
            pl.BlockSpec((1, d), lambda bi: (0, 0)),
            pl.BlockSpec((d, 2 * BR_W), lambda bi: (0, 0)),
            pl.BlockSpec((1, HEAD_D), lambda bi: (0, 0)),
        ],
        out_specs=[pl.BlockSpec((None, m, BR_W), lambda bi: (bi, 0, 0))] * 2,
        out_shape=[jax.ShapeDtypeStruct((b, m, BR_W), BF16)] * 2,
        compiler_params=pltpu.CompilerParams(
            dimension_semantics=("arbitrary",), vmem_limit_bytes=_vmem_limit(est)),
        name="mem_kv",
    )(mem, g, w, gk)


def _mem_attn_kernel(q_ref, k_ref, v_ref, gq_ref, o_ref):
    scale = HEAD_D ** -0.5
    gq = gq_ref[...]
    for h in range(N_HEAD):
        sl = slice(h * HEAD_D, (h + 1) * HEAD_D)
        qn = _rms_rows(q_ref[:, sl], gq).astype(BF16)
        sc = _dot_nt(qn, k_ref[:, sl]) * scale
        e = jnp.exp(sc - jnp.max(sc, axis=-1, keepdims=True))
        o = _dot(e.astype(BF16), v_ref[:, sl])
        o_ref[:, sl] = (o / jnp.sum(e, axis=-1, keepdims=True)).astype(o_ref.dtype)


def _mem_attn(mq3, km, vm, gq, ts):
    b, s, _ = mq3.shape
    m = km.shape[1]
    est = 2 * ts * BR_W * 4 + 4 * m * BR_W * 2 + 2 * ts * BR_W * 2 + 6 * ts * m * 4
    return pl.pallas_call(
        _mem_attn_kernel,
        grid=(b, s // ts),
        in_specs=[
            pl.BlockSpec((None, ts, BR_W), lambda bi, si: (bi, si, 0)),
            pl.BlockSpec((None, m, BR_W), lambda bi, si: (bi, 0, 0)),
            pl.BlockSpec((None, m, BR_W), lambda bi, si: (bi, 0, 0)),
            pl.BlockSpec((1, HEAD_D), lambda bi, si: (0, 0)),
        ],
        out_specs=pl.BlockSpec((None, ts, BR_W), lambda bi, si: (bi, si, 0)),
        out_shape=jax.ShapeDtypeStruct((b, s, BR_W), BF16),
        compiler_params=pltpu.CompilerParams(
            dimension_semantics=("arbitrary", "arbitrary"), vmem_limit_bytes=_vmem_limit(est)),
        name="mem_attn",
    )(mq3, km, vm, gq)


FF_CHUNK = 1024


def _merge_mlp_kernel(x_ref, gate_ref, og_ref, os_ref, om_ref, wg_ref, ws_ref, wm_ref, wo_ref,
                      g2_ref, wu_ref, wd_ref, o_ref):
    d = x_ref.shape[1]
    mix = None
    for j, (b_ref, w_ref) in enumerate(((og_ref, wg_ref), (os_ref, ws_ref), (om_ref, wm_ref))):
        term = gate_ref[:, j * d:(j + 1) * d].astype(F32) * _dot(b_ref[...], w_ref[...])
        mix = term if mix is None else mix + term
    x1 = x_ref[...] + _dot(mix.astype(BF16), wo_ref[...])
    hn = _rms_rows(x1, g2_ref[...]).astype(BF16)
    acc = x1
    for c0 in range(0, wu_ref.shape[1], FF_CHUNK):
        up = _dot(hn, wu_ref[:, c0:c0 + FF_CHUNK])
        act = jnp.square(jnp.maximum(up, 0.0)).astype(BF16)
        acc = acc + _dot(act, wd_ref[c0:c0 + FF_CHUNK, :])
    o_ref[...] = acc


def _merge_mlp(x2, gates, og, osb, om, wg, ws, wm, wo, g2, wu, wd, tm):
    t, d = x2.shape
    ff = wu.shape[1]
    est = (2 * tm * d * 4 + 2 * tm * 3 * d * 2 + 6 * tm * BR_W * 2 + (3 * BR_W * d + d * d + 2 * d * ff) * 2
           + 2 * tm * d * 4 + 6 * tm * d * 4 + 3 * tm * FF_CHUNK * 4)
    wspec = lambda shape: pl.BlockSpec(shape, lambda i: (0, 0), pipeline_mode=pl.Buffered(1))
    tile = lambda n: pl.BlockSpec((tm, n), lambda i: (i, 0))
    return pl.pallas_call(
        _merge_mlp_kernel,
        grid=(t // tm,),
        in_specs=[
            tile(d), tile(3 * d), tile(BR_W), tile(BR_W), tile(BR_W),
            wspec((BR_W, d)), wspec((BR_W, d)), wspec((BR_W, d)), wspec((d, d)),
            pl.BlockSpec((1, d), lambda i: (0, 0)), wspec((d, ff)), wspec((ff, d)),
        ],
        out_specs=tile(d),
        out_shape=jax.ShapeDtypeStruct((t, d), F32),
        compiler_params=pltpu.CompilerParams(
            dimension_semantics=("arbitrary",), vmem_limit_bytes=_vmem_limit(est)),
        name="merge_mlp",
    )(x2, gates, og, osb, om, wg, ws, wm, wo, g2, wu, wd)


def _pick(n, pref):
    t = min(n, pref)
    while n % t:
        t //= 2
    return t


def _layer(x, mem, norm1_g, w_in, conv_w, a_log, dt_bias, gdn_norm_g, sb_q_norm_g, sb_k_norm_g,
           mem_norm_g, w_mem_kv, mem_q_norm_g, mem_k_norm_g, w_br_gdn, w_br_sb, w_br_mem, w_o,
           norm2_g, w_up, w_down):
    b, s, d = x.shape
    t = b * s
    x2 = x.reshape(t, d)
    row = lambda v: v.reshape(1, -1).astype(F32)

    n_ab = 2 * N_HEAD
    w_lo = w_in[:, :4 * BR_W].astype(BF16)
    w_ab = jnp.pad(w_in[:, 4 * BR_W:4 * BR_W + n_ab], ((0, 0), (0, LANE - n_ab))).astype(BF16)
    w_hi = w_in[:, 4 * BR_W + n_ab:].astype(BF16)
    outs = ((3 * d, BF16, "sigmoid", 2, 4 * BR_W), (3 * BR_W, F32, "gdn_qkv", 0, 0),
            (BR_W, BF16, 0, 2, 0), (BR_W, BF16, 1, 2, BR_W), (BR_W, BF16, None, 2, 2 * BR_W),
            (BR_W, F32, None, 0, 3 * BR_W), (BR_W, F32, None, 2, 3 * BR_W), (LANE, F32, None, 1, 0))
    head_gains = jnp.zeros((8, HEAD_D), F32).at[0].set(sb_q_norm_g.astype(F32)).at[1].set(sb_k_norm_g.astype(F32))
    gates, gqkv, sq, sk, sv, gz, mq, gab = _in_proj(
        x2, row(norm1_g), head_gains, conv_w.astype(F32), (w_lo, w_ab, w_hi), outs, _pick(s, 512), s)

    gp = jnp.zeros((8, LANE), F32)
    gp = gp.at[0, :N_HEAD].set(a_log.astype(F32)).at[1, :N_HEAD].set(dt_bias.astype(F32))
    o_gdn = _gdn(gqkv, gz, gab, gp, row(gdn_norm_g), b, s, _pick(s, 512))

    to3 = lambda a: a.reshape(b, s, BR_W)
    o_sb = _sb_attn(to3(sq), to3(sk), to3(sv), _pick(s // SB_BLK, 4))

    km, vm = _mem_kv(mem, row(mem_norm_g), w_mem_kv.astype(BF16), row(mem_k_norm_g))
    o_mem = _mem_attn(mq.reshape(b, s, BR_W), km, vm, row(mem_q_norm_g), _pick(s, 512))

    out = _merge_mlp(x2, gates, o_gdn, o_sb.reshape(t, BR_W), o_mem.reshape(t, BR_W),
                     w_br_gdn.astype(BF16), w_br_sb.astype(BF16), w_br_mem.astype(BF16),
                     w_o.astype(BF16), row(norm2_g), w_up.astype(BF16), w_down.astype(BF16), _pick(t, 512))
    return out.reshape(b, s, d)


def kernel(x, mem, norm1_g, w_in, conv_w, a_log, dt_bias, gdn_norm_g, sb_q_norm_g, sb_k_norm_g,
           mem_norm_g, w_mem_kv, mem_q_norm_g, mem_k_norm_g, w_br_gdn, w_br_sb, w_br_mem, w_o,
           norm2_g, w_up, w_down):
    depth = w_in.shape[0]
    for l in range(depth):
        x = _layer(x, mem, norm1_g[l], w_in[l], conv_w[l], a_log[l], dt_bias[l], gdn_norm_g[l],
                   sb_q_norm_g[l], sb_k_norm_g[l], mem_norm_g[l], w_mem_kv[l], mem_q_norm_g[l],
                   mem_k_norm_g[l], w_br_gdn[l], w_br_sb[l], w_br_mem[l], w_o[l], norm2_g[l],
                   w_up[l], w_down[l])
    return x
```

```python
import functools

import jax
import jax.numpy as jnp
from jax import lax
from jax.experimental import pallas as pl
from jax.experimental.pallas import tpu as pltpu

F32 = jnp.float32
BF16 = jnp.bfloat16

EPS = 1e-6
N_HEAD = 4
HEAD_D = 128
BR_W = N_HEAD * HEAD_D
CONV_W = 4
CHUNK = 64
LANE = 128
V7X_VMEM_BYTES = 64 * 1024 * 1024
EXP_ZERO_BELOW = -104.0


def _vmem_limit(estimate_bytes):
    return int(min(estimate_bytes * 1.25 + (4 << 20), V7X_VMEM_BYTES - (6 << 20)))


def _dot(a, b):
    return jnp.dot(a, b, preferred_element_type=F32)


def _dot_nt(a, b):
    return lax.dot_general(a, b, (((1,), (1,)), ((), ())), preferred_element_type=F32)


def _dot_tn(a, b):
    return lax.dot_general(a, b, (((0,), (0,)), ((), ())), preferred_element_type=F32)


def _split(a):
    hi = a.astype(BF16)
    lo = (a - hi.astype(F32)).astype(BF16)
    return hi, lo


def _dot_exact_lhs(m_bf16, b):
    hi, lo = _split(b)
    return _dot(m_bf16, hi) + _dot(m_bf16, lo)


def _dot_exact_rhs(a, m_bf16):
    hi, lo = _split(a)
    return _dot(hi, m_bf16) + _dot(lo, m_bf16)


def _silu(x):
    return (0.5 * x) * (1.0 + jnp.tanh(0.5 * x))


def _sigmoid(x):
    return 0.5 + 0.5 * jnp.tanh(0.5 * x)


def _softplus(x):
    return jnp.maximum(x, 0.0) + jnp.log1p(jnp.exp(-jnp.abs(x)))


def _rms_rows(x, g_row):
    ms = jnp.mean(x * x, axis=-1, keepdims=True)
    return x * lax.rsqrt(ms + EPS) * g_row


IN_COL_CHUNK = 512
CONV_HALO = 8


def _in_proj_kernel(outs, n_w, tiles_per_seq, x_ref, g_ref, hg_ref, cw_ref, *refs):
    w_refs, out_refs, xp_ref = refs[:n_w], refs[n_w:n_w + len(outs)], refs[n_w + len(outs)]
    tm = x_ref.shape[0]

    @pl.when(pl.program_id(0) % tiles_per_seq == 0)
    def _():
        xp_ref[0:CONV_HALO, :] = jnp.zeros((CONV_HALO, xp_ref.shape[1]), F32)

    def conv_head(o_ref, h0):
        cols = slice(h0, h0 + HEAD_D)
        xe = xp_ref[:, cols]
        acc = None
        for j in range(CONV_W):
            shift = CONV_W - 1 - j
            rows = (pltpu.roll(xe, shift, axis=0) if shift else xe)[CONV_HALO:]
            term = cw_ref[j:j + 1, cols] * rows
            acc = term if acc is None else acc + term
        xp_ref[0:CONV_HALO, cols] = xe[tm:]
        a = _silu(acc)
        if h0 < 2 * BR_W:
            a = a * lax.rsqrt(jnp.sum(a * a, axis=-1, keepdims=True) + EPS)
        if h0 < BR_W:
            a = a * (HEAD_D ** -0.5)
        o_ref[:, cols] = a.astype(o_ref.dtype)

    hn = _rms_rows(x_ref[...], g_ref[...]).astype(BF16)
    for o_ref, (n, dtype, post, wi, wcol) in zip(out_refs, outs):
        for c0 in range(0, n, IN_COL_CHUNK):
            cw = min(IN_COL_CHUNK, n - c0)
            cols = slice(c0, c0 + cw)
            r = _dot(hn, w_refs[wi][:, wcol + c0:wcol + c0 + cw])
            if post == "gdn_qkv":
                xp_ref[CONV_HALO:CONV_HALO + tm, cols] = r
                for h0 in range(0, cw, HEAD_D):
                    conv_head(o_ref, c0 + h0)
                continue
            if post is None:
                o_ref[:, cols] = r.astype(dtype)
            elif post == "sigmoid":
                o_ref[:, cols] = _sigmoid(r).astype(dtype)
            else:
                gain = hg_ref[post:post + 1, :]
                for h0 in range(0, cw, HEAD_D):
                    o_ref[:, c0 + h0:c0 + h0 + HEAD_D] = _rms_rows(r[:, h0:h0 + HEAD_D], gain).astype(dtype)


def _in_proj(x2, g, head_gains, conv_w, ws, outs, tm, seq):
    t, d = x2.shape
    w3 = conv_w.shape[1]
    est = (2 * tm * d * 4 + sum(d * w.shape[1] * 2 for w in ws)
           + 2 * tm * sum(o[0] * jnp.dtype(o[1]).itemsize for o in outs)
           + (tm + CONV_HALO) * w3 * 4 + 8 * tm * IN_COL_CHUNK * 4)
    return pl.pallas_call(
        functools.partial(_in_proj_kernel, outs, len(ws), seq // tm),
        grid=(t // tm,),
        in_specs=[
            pl.BlockSpec((tm, d), lambda i: (i, 0)),
            pl.BlockSpec((1, d), lambda i: (0, 0)),
            pl.BlockSpec(head_gains.shape, lambda i: (0, 0)),
            pl.BlockSpec(conv_w.shape, lambda i: (0, 0)),
        ] + [pl.BlockSpec(w.shape, lambda i: (0, 0), pipeline_mode=pl.Buffered(1)) for w in ws],
        out_specs=[pl.BlockSpec((tm, o[0]), lambda i: (i, 0)) for o in outs],
        out_shape=[jax.ShapeDtypeStruct((t, o[0]), o[1]) for o in outs],
        scratch_shapes=[pltpu.VMEM((tm + CONV_HALO, w3), F32)],
        compiler_params=pltpu.CompilerParams(
            dimension_semantics=("arbitrary",), vmem_limit_bytes=_vmem_limit(est)),
        name="in_proj",
    )(x2, g, head_gains, conv_w, *ws)


def _dot3_pairs(a, b):
    m = a[0].shape[0]
    both = _dot(jnp.concatenate([a[0], a[1]], axis=0), b[0])
    return both[:m] + (both[m:] + _dot(a[0], b[1]))


def _pair_masks():
    row = lax.broadcasted_iota(jnp.int32, (CHUNK, 2 * CHUNK), 0)
    lane = lax.broadcasted_iota(jnp.int32, (CHUNK, 2 * CHUNK), 1)
    col = lane & (CHUNK - 1)
    return dict(first=lane < CHUNK, strict=row > col, causal=row >= col, eye=row == col,
                m8=(row >> 3) == (col >> 3), m16=(row >> 4) == (col >> 4), m32=(row >> 5) == (col >> 5))


def _block_diag(x, first):
    zero = jnp.zeros_like(x)
    return jnp.concatenate([jnp.where(first, x, zero), jnp.where(first, zero, x)], axis=0)


def _tri_inverse_pairs(lowers, mk):
    right = lambda hl: (_block_diag(hl[0], mk["first"]), _block_diag(hl[1], mk["first"]))
    eye = mk["eye"].astype(F32)
    c16 = mk["m16"] & jnp.logical_not(mk["m8"])
    c32 = mk["m32"] & jnp.logical_not(mk["m16"])
    ld = [jnp.where(mk["m8"], l, 0.0) for l in lowers]
    p = [eye - x for x in ld]
    a_l = [_split(x) for x in ld]
    a_r = [right(x) for x in a_l]
    for _ in range(2):
        a_l = [_split(_dot3_pairs(xl, xr)) for xl, xr in zip(a_l, a_r)]
        a_r = [right(x) for x in a_l]
        p = [pi + _dot3_pairs(_split(pi), xr) for pi, xr in zip(p, a_r)]
    for sel in (lambda l: jnp.where(c16, l, 0.0), lambda l: jnp.where(c32, l, 0.0),
                lambda l: jnp.where(mk["m32"], 0.0, l)):
        p16 = [pi.astype(BF16) for pi in p]
        p_r = [_block_diag(x, mk["first"]) for x in p16]
        t = [_block_diag(_dot(sel(l).astype(BF16), xr).astype(BF16), mk["first"]) for l, xr in zip(lowers, p_r)]
        p = [pi - _dot(xl, ti) for pi, xl, ti in zip(p, p16, t)]
    return p


def _gdn_kernel(ts, q_s, k_s, v_s, z_ref, ab_ref, gp_ref, ng_ref, o_ref, g_s, b_s, state_ref):
    @pl.when(pl.program_id(1) == 0)
    def _():
        state_ref[...] = jnp.zeros_like(state_ref)

    ab = ab_ref[...]
    g_all = -jnp.exp(gp_ref[0:1, :]) * _softplus(ab + gp_ref[1:2, :])
    beta_all = _sigmoid(ab)
    for h in range(N_HEAD):
        sl = slice(h * HEAD_D, (h + 1) * HEAD_D)
        g_s[:, sl] = jnp.broadcast_to(g_all[:, h:h + 1], (ts, HEAD_D))
        b_s[:, sl] = jnp.broadcast_to(beta_all[:, N_HEAD + h:N_HEAD + h + 1], (ts, HEAD_D))

    mk = _pair_masks()
    tri = (lax.broadcasted_iota(jnp.int32, (CHUNK, CHUNK), 0)
           >= lax.broadcasted_iota(jnp.int32, (CHUNK, CHUNK), 1)).astype(BF16)
    ng = ng_ref[...]
    zero_blk = jnp.zeros((CHUNK, HEAD_D), BF16)

    def blk(ref, c, h):
        return ref[c * CHUNK:(c + 1) * CHUNK, h * HEAD_D:(h + 1) * HEAD_D]

    def block_diag2(a, b):
        return jnp.concatenate([jnp.concatenate([a, zero_blk], axis=1),
                                jnp.concatenate([zero_blk, b], axis=1)], axis=0)

    nc = ts // CHUNK
    probs = [(c, h) for c in range(nc) for h in range(0, N_HEAD, 2)]

    gcb = [_dot_exact_lhs(tri, g_s[c * CHUNK:(c + 1) * CHUNK, h * HEAD_D:(h + 2) * HEAD_D]) for c, h in probs]
    decay = [jnp.exp(_dot_exact_lhs(tri, jnp.where(
        mk["strict"], jnp.where(mk["first"], blk(g_s, c, h), blk(g_s, c, h + 1)), 0.0))) for c, h in probs]
    kk, qk = [], []
    for c, h in probs:
        kk2, qk2 = [], []
        for hh in (h, h + 1):
            k = blk(k_s, c, hh)
            k16 = k.astype(BF16)
            kk2.append(_dot_nt((k * blk(b_s, c, hh)).astype(BF16), k16))
            qk2.append(_dot_nt(blk(q_s, c, hh).astype(BF16), k16))
        kk.append(jnp.concatenate(kk2, axis=1))
        qk.append(jnp.concatenate(qk2, axis=1))
    lower = [jnp.where(mk["strict"], x * dc, 0.0) for x, dc in zip(kk, decay)]
    a_qk = [jnp.where(mk["causal"], x * dc, 0.0).astype(BF16) for x, dc in zip(qk, decay)]
    t_inv = [x.astype(BF16) for x in _tri_inverse_pairs(lower, mk)]
    wu, q_dec, k_dec, g_state = {}, {}, {}, {}
    for (c, h), ti, gc in zip(probs, t_inv, gcb):
        beta = [blk(b_s, c, hh) for hh in (h, h + 1)]
        eg = jnp.exp(gc)
        egs = (eg[:, :HEAD_D], eg[:, HEAD_D:])
        u2 = _dot(ti, block_diag2(*[(blk(v_s, c, hh) * bt).astype(BF16) for hh, bt in zip((h, h + 1), beta)]))
        w2 = _dot(ti, block_diag2(*[(blk(k_s, c, hh) * bt * e).astype(BF16)
                                    for hh, bt, e in zip((h, h + 1), beta, egs)]))
        for n, hh in enumerate((h, h + 1)):
            lanes = slice(n * HEAD_D, (n + 1) * HEAD_D)
            wu[c, hh] = jnp.concatenate([w2[:, lanes].astype(BF16), u2[:, lanes].astype(BF16)], axis=1)
            q_dec[c, hh] = blk(q_s, c, hh) * egs[n]
            g_last = gc[CHUNK - 1:CHUNK, lanes]
            k_dec[c, hh] = (blk(k_s, c, hh) * jnp.exp(g_last - gc[:, lanes])).astype(BF16)
            g_state[c, hh] = jnp.exp(g_last)
    k_wu = {p: _dot_tn(k_dec[p], wu[p]) for p in wu}
    zero_wu = jnp.zeros((CHUNK, 2 * HEAD_D), BF16)
    lhs, k_u, a_u = {}, {}, {}
    for (c, h), aq in zip(probs, a_qk):
        bd = jnp.concatenate([jnp.concatenate([wu[c, h], zero_wu], axis=1),
                              jnp.concatenate([zero_wu, wu[c, h + 1]], axis=1)], axis=0)
        a_wu = _dot(aq, bd)
        for n, hh in enumerate((h, h + 1)):
            a_w = a_wu[:, 2 * n * HEAD_D:(2 * n + 1) * HEAD_D]
            a_u[c, hh] = a_wu[:, (2 * n + 1) * HEAD_D:(2 * n + 2) * HEAD_D]
            k_u[c, hh] = k_wu[c, hh][:, HEAD_D:]
            lhs[c, hh] = jnp.concatenate([(-k_wu[c, hh][:, :HEAD_D]).astype(BF16),
                                          (q_dec[c, hh] - a_w).astype(BF16)], axis=0)

    state = [state_ref[h] for h in range(N_HEAD)]
    for c in range(nc):
        r = [_dot(lhs[c, h], state[h].astype(BF16)) for h in range(N_HEAD)]
        for h in range(N_HEAD):
            o = r[h][HEAD_D:] + a_u[c, h]
            rows = slice(c * CHUNK, (c + 1) * CHUNK)
            sl = slice(h * HEAD_D, (h + 1) * HEAD_D)
            o_ref[rows, sl] = (_rms_rows(o, ng) * _silu(z_ref[rows, sl])).astype(o_ref.dtype)
        state = [state[h] * g_state[c, h] + (k_u[c, h] + r[h][:HEAD_D]) for h in range(N_HEAD)]
    for h in range(N_HEAD):
        state_ref[h] = state[h]


def _gdn(gqkv, gz, gab, gp, ng, b, s, ts):
    nsb = s // ts
    est = (2 * ts * (4 * BR_W + LANE) * 4 + 2 * ts * BR_W * 2 + 2 * ts * BR_W * 4
           + N_HEAD * HEAD_D * HEAD_D * 4 + 24 * ts * BR_W * 4)
    tok = lambda n, j: pl.BlockSpec((ts, n), lambda bi, si: (bi * nsb + si, j))
    return pl.pallas_call(
        functools.partial(_gdn_kernel, ts),
        grid=(b, nsb),
        in_specs=[
            tok(BR_W, 0), tok(BR_W, 1), tok(BR_W, 2),
            tok(BR_W, 0), tok(LANE, 0),
            pl.BlockSpec((8, LANE), lambda bi, si: (0, 0)),
            pl.BlockSpec((1, HEAD_D), lambda bi, si: (0, 0)),
        ],
        out_specs=tok(BR_W, 0),
        out_shape=jax.ShapeDtypeStruct((b * s, BR_W), BF16),
        scratch_shapes=[
            pltpu.VMEM((ts, BR_W), F32),
            pltpu.VMEM((ts, BR_W), F32),
            pltpu.VMEM((N_HEAD, HEAD_D, HEAD_D), F32),
        ],
        compiler_params=pltpu.CompilerParams(
            dimension_semantics=("arbitrary", "arbitrary"), vmem_limit_bytes=_vmem_limit(est)),
        name="gdn",
    )(gqkv, gqkv, gqkv, gz, gab, gp, ng)


SB_BLK = 128
SB_HEAD = 32
LOG2E = 1.4426950408889634


def _sb_kernel(nsub, q_ref, k_ref, v_ref, o_ref, acc_ref, c_ref):
    i = pl.program_id(1)
    scale2 = (HEAD_D ** -0.5) * LOG2E
    dead_below = EXP_ZERO_BELOW * LOG2E
    row = lax.broadcasted_iota(jnp.int32, (SB_BLK, SB_BLK), 0)
    col = lax.broadcasted_iota(jnp.int32, (SB_BLK, SB_BLK), 1)
    before = col < row
    row2 = lax.broadcasted_iota(jnp.int32, (SB_BLK, 2 * SB_BLK), 0)
    col2 = lax.broadcasted_iota(jnp.int32, (SB_BLK, 2 * SB_BLK), 1)
    sum_rhs = jnp.logical_or(row2 > col2, col2 >= SB_BLK).astype(BF16)
    lanes = lambda h: slice(h * HEAD_D, (h + 1) * HEAD_D)

    def step(t, first):
        visits = []
        for j in range(nsub):
            for h in range(N_HEAD):
                kb = nsub * i + j - t
                visits.append((j, h, 0 if first else SB_HEAD, SB_BLK, kb, first))
                visits.append((j, h, 0, SB_HEAD, kb - 1, False))
        k0s, zs = [], []
        for j, h, r0, r1, kb, diagonal in visits:
            k0 = pl.multiple_of(jnp.maximum(kb, 0) * SB_BLK, SB_BLK)
            z = _dot_nt(q_ref[j * SB_BLK + r0:j * SB_BLK + r1, lanes(h)],
                        k_ref[pl.ds(k0, SB_BLK), lanes(h)]) * scale2
            if not diagonal:
                z = z + jnp.where(kb >= 0, 0.0, -jnp.inf)
            k0s.append(k0)
            zs.append(z)
        log_1mb = [-(jnp.maximum(z, 0.0) + jnp.log2(1.0 + jnp.exp2(-jnp.abs(z)))) for z in zs]
        log_1mb = [jnp.where(before, x, 0.0) if v[5] else x for v, x in zip(visits, log_1mb)]
        sums = [_dot_exact_rhs(x, sum_rhs) for x in log_1mb]
        after_diag = {}
        m_head, m_rest = None, None
        for (j, h, r0, r1, kb, diagonal), k0, z, l1, sm in zip(visits, k0s, zs, log_1mb, sums):
            rows = slice(j * SB_BLK + r0, j * SB_BLK + r1)
            between = sm[:, :SB_BLK]
            c_new = sm[:, SB_BLK:]
            if not diagonal:
                c = after_diag[j, h][:SB_HEAD] if first else c_ref[rows, lanes(h)]
                between = between + c
                c_new = c_new + c
            att = jnp.exp2(z + l1 + between)
            if diagonal:
                att = jnp.where(before, att, 0.0)
            pv = _dot(att.astype(BF16), v_ref[pl.ds(k0, SB_BLK), lanes(h)])
            if diagonal:
                acc_ref[rows, lanes(h)] = pv
                after_diag[j, h] = c_new
                c_new = c_new[SB_HEAD:]
                rows = slice(j * SB_BLK + SB_HEAD, j * SB_BLK + SB_BLK)
            else:
                acc_ref[rows, lanes(h)] += pv
            c_ref[rows, lanes(h)] = c_new
            if r1 == SB_HEAD:
                m_head = c_new if m_head is None else jnp.maximum(m_head, c_new)
            else:
                m_rest = c_new if m_rest is None else jnp.maximum(m_rest, c_new)
        return jnp.maximum(jnp.max(m_head), jnp.max(m_rest))

    def cond(carry):
        t, cmax = carry
        return jnp.logical_and(t <= nsub * i + (nsub - 1), cmax > dead_below)

    def body(carry):
        t, _ = carry
        return t + 1, step(t, False)

    lax.while_loop(cond, body, (jnp.int32(1), step(0, True)))
    o_ref[...] = acc_ref[...].astype(o_ref.dtype)


def _sb_attn(sq3, sk3, sv3, nsub):
    b, s, _ = sq3.shape
    tq = nsub * SB_BLK
    est = 4 * tq * BR_W * 2 + 4 * s * BR_W * 2 + 2 * tq * BR_W * 4 + 16 * nsub * N_HEAD * SB_BLK * SB_BLK * 4
    kv_spec = pl.BlockSpec((None, s, BR_W), lambda bi, i: (bi, 0, 0))
    return pl.pallas_call(
        functools.partial(_sb_kernel, nsub),
        grid=(b, s // tq),
        in_specs=[pl.BlockSpec((None, tq, BR_W), lambda bi, i: (bi, i, 0)), kv_spec, kv_spec],
        out_specs=pl.BlockSpec((None, tq, BR_W), lambda bi, i: (bi, i, 0)),
        out_shape=jax.ShapeDtypeStruct((b, s, BR_W), BF16),
        scratch_shapes=[pltpu.VMEM((tq, BR_W), F32), pltpu.VMEM((tq, BR_W), F32)],
        compiler_params=pltpu.CompilerParams(
            dimension_semantics=("arbitrary", "arbitrary"), vmem_limit_bytes=_vmem_limit(est)),
        name="sb_attn",
    )(sq3, sk3, sv3)


def _mem_kv_kernel(mem_ref, g_ref, w_ref, gk_ref, k_ref, v_ref):
    hn = _rms_rows(mem_ref[...], g_ref[...]).astype(BF16)
    kv = _dot(hn, w_ref[...])
    gk = gk_ref[...]
    for h in range(N_HEAD):
        sl = slice(h * HEAD_D, (h + 1) * HEAD_D)
        k_ref[:, sl] = _rms_rows(kv[:, sl], gk).astype(k_ref.dtype)
    v_ref[...] = kv[:, BR_W:].astype(v_ref.dtype)


def _mem_kv(mem, g, w, gk):
    b, m, d = mem.shape
    est = 2 * m * d * 4 + 2 * d * 2 * BR_W * 2 + 4 * m * BR_W * 2 + 4 * m * 2 * BR_W * 4
    return pl.pallas_call(
        _mem_kv_kernel,
        grid=(b,),
        in_specs=[
            pl.BlockSpec((None, m, d), lambda bi: (bi, 0, 0)),
            pl.BlockSpec((1, d), lambda bi: (0, 0)),
            pl.BlockSpec((d, 2 * BR_W), lambda bi: (0, 0)),
            pl.BlockSpec((1, HEAD_D), lambda bi: (0, 0)),
        ],
        out_specs=[pl.BlockSpec((None, m, BR_W), lambda bi: (bi, 0, 0))] * 2,
        out_shape=[jax.ShapeDtypeStruct((b, m, BR_W), BF16)] * 2,
        compiler_params=pltpu.CompilerParams(
            dimension_semantics=("arbitrary",), vmem_limit_bytes=_vmem_limit(est)),
        name="mem_kv",
    )(mem, g, w, gk)


def _mem_attn_kernel(q_ref, k_ref, v_ref, gq_ref, o_ref):
    scale = HEAD_D ** -0.5
    gq = gq_ref[...]
    for h in range(N_HEAD):
        sl = slice(h * HEAD_D, (h + 1) * HEAD_D)
        qn = _rms_rows(q_ref[:, sl], gq).astype(BF16)
        sc = _dot_nt(qn, k_ref[:, sl]) * scale
        e = jnp.exp(sc - jnp.max(sc, axis=-1, keepdims=True))
        o = _dot(e.astype(BF16), v_ref[:, sl])
        o_ref[:, sl] = (o / jnp.sum(e, axis=-1, keepdims=True)).astype(o_ref.dtype)


def _mem_attn(mq3, km, vm, gq, ts):
    b, s, _ = mq3.shape
    m = km.shape[1]
    est = 2 * ts * BR_W * 4 + 4 * m * BR_W * 2 + 2 * ts * BR_W * 2 + 6 * ts * m * 4
    return pl.pallas_call(
        _mem_attn_kernel,
        grid=(b, s // ts),
        in_specs=[
            pl.BlockSpec((None, ts, BR_W), lambda bi, si: (bi, si, 0)),
            pl.BlockSpec((None, m, BR_W), lambda bi, si: (bi, 0, 0)),
            pl.BlockSpec((None, m, BR_W), lambda bi, si: (bi, 0, 0)),
            pl.BlockSpec((1, HEAD_D), lambda bi, si: (0, 0)),
        ],
        out_specs=pl.BlockSpec((None, ts, BR_W), lambda bi, si: (bi, si, 0)),
        out_shape=jax.ShapeDtypeStruct((b, s, BR_W), BF16),
        compiler_params=pltpu.CompilerParams(
            dimension_semantics=("arbitrary", "arbitrary"), vmem_limit_bytes=_vmem_limit(est)),
        name="mem_attn",
    )(mq3, km, vm, gq)


FF_CHUNK = 1024


def _merge_mlp_kernel(x_ref, gate_ref, og_ref, os_ref, om_ref, wg_ref, ws_ref, wm_ref, wo_ref,
                      g2_ref, wu_ref, wd_ref, o_ref):
    d = x_ref.shape[1]
    mix = None
    for j, (b_ref, w_ref) in enumerate(((og_ref, wg_ref), (os_ref, ws_ref), (om_ref, wm_ref))):
        term = gate_ref[:, j * d:(j + 1) * d].astype(F32) * _dot(b_ref[...], w_ref[...])
        mix = term if mix is None else mix + term
    x1 = x_ref[...] + _dot(mix.astype(BF16), wo_ref[...])
    hn = _rms_rows(x1, g2_ref[...]).astype(BF16)
    acc = x1
    for c0 in range(0, wu_ref.shape[1], FF_CHUNK):
        up = _dot(hn, wu_ref[:, c0:c0 + FF_CHUNK])
        act = jnp.square(jnp.maximum(up, 0.0)).astype(BF16)
        acc = acc + _dot(act, wd_ref[c0:c0 + FF_CHUNK, :])
    o_ref[...] = acc


def _merge_mlp(x2, gates, og, osb, om, wg, ws, wm, wo, g2, wu, wd, tm):
    t, d = x2.shape
    ff = wu.shape[1]
    est = (2 * tm * d * 4 + 2 * tm * 3 * d * 2 + 6 * tm * BR_W * 2 + (3 * BR_W * d + d * d + 2 * d * ff) * 2
           + 2 * tm * d * 4 + 6 * tm * d * 4 + 3 * tm * FF_CHUNK * 4)
    wspec = lambda shape: pl.BlockSpec(shape, lambda i: (0, 0), pipeline_mode=pl.Buffered(1))
    tile = lambda n: pl.BlockSpec((tm, n), lambda i: (i, 0))
    return pl.pallas_call(
        _merge_mlp_kernel,
        grid=(t // tm,),
        in_specs=[
            tile(d), tile(3 * d), tile(BR_W), tile(BR_W), tile(BR_W),
            wspec((BR_W, d)), wspec((BR_W, d)), wspec((BR_W, d)), wspec((d, d)),
            pl.BlockSpec((1, d), lambda i: (0, 0)), wspec((d, ff)), wspec((ff, d)),
        ],
        out_specs=tile(d),
        out_shape=jax.ShapeDtypeStruct((t, d), F32),
        compiler_params=pltpu.CompilerParams(
            dimension_semantics=("arbitrary",), vmem_limit_bytes=_vmem_limit(est)),
        name="merge_mlp",
    )(x2, gates, og, osb, om, wg, ws, wm, wo, g2, wu, wd)


def _pick(n, pref):
    t = min(n, pref)
    while n % t:
        t //= 2
    return t


def _layer(x, mem, norm1_g, w_in, conv_w, a_log, dt_bias, gdn_norm_g, sb_q_norm_g, sb_k_norm_g,
           mem_norm_g, w_mem_kv, mem_q_norm_g, mem_k_norm_g, w_br_gdn, w_br_sb, w_br_mem, w_o,
           norm2_g, w_up, w_down):
    b, s, d = x.shape
    t = b * s
    x2 = x.reshape(t, d)
    row = lambda v: v.reshape(1, -1).astype(F32)

    n_ab = 2 * N_HEAD
    w_lo = w_in[:, :4 * BR_W].astype(BF16)
    w_ab = jnp.pad(w_in[:, 4 * BR_W:4 * BR_W + n_ab], ((0, 0), (0, LANE - n_ab))).astype(BF16)
    w_hi = w_in[:, 4 * BR_W + n_ab:].astype(BF16)
    outs = ((3 * d, BF16, "sigmoid", 2, 4 * BR_W), (3 * BR_W, F32, "gdn_qkv", 0, 0),
            (BR_W, BF16, 0, 2, 0), (BR_W, BF16, 1, 2, BR_W), (BR_W, BF16, None, 2, 2 * BR_W),
            (BR_W, F32, None, 0, 3 * BR_W), (BR_W, F32, None, 2, 3 * BR_W), (LANE, F32, None, 1, 0))
    head_gains = jnp.zeros((8, HEAD_D), F32).at[0].set(sb_q_norm_g.astype(F32)).at[1].set(sb_k_norm_g.astype(F32))
    gates, gqkv, sq, sk, sv, gz, mq, gab = _in_proj(
        x2, row(norm1_g), head_gains, conv_w.astype(F32), (w_lo, w_ab, w_hi), outs, _pick(s, 512), s)

    gp = jnp.zeros((8, LANE), F32)
    gp = gp.at[0, :N_HEAD].set(a_log.astype(F32)).at[1, :N_HEAD].set(dt_bias.astype(F32))
    o_gdn = _gdn(gqkv, gz, gab, gp, row(gdn_norm_g), b, s, _pick(s, 512))

    to3 = lambda a: a.reshape(b, s, BR_W)
    o_sb = _sb_attn(to3(sq), to3(sk), to3(sv), _pick(s // SB_BLK, 8))

    km, vm = _mem_kv(mem, row(mem_norm_g), w_mem_kv.astype(BF16), row(mem_k_norm_g))
    o_mem = _mem_attn(mq.reshape(b, s, BR_W), km, vm, row(mem_q_norm_g), _pick(s, 512))

    out = _merge_mlp(x2, gates, o_gdn, o_sb.reshape(t, BR_W), o_mem.reshape(t, BR_W),
                     w_br_gdn.astype(BF16), w_br_sb.astype(BF16), w_br_mem.astype(BF16),
                     w_o.astype(BF16), row(norm2_g), w_up.astype(BF16), w_down.astype(BF16), _pick(t, 512))
    return out.reshape(b, s, d)


def kernel(x, mem, norm1_g, w_in, conv_w, a_log, dt_bias, gdn_norm_g, sb_q_norm_g, sb_k_norm_g,
           mem_norm_g, w_mem_kv, mem_q_norm_g, mem_k_norm_g, w_br_gdn, w_br_sb, w_br_mem, w_o,
           norm2_g, w_up, w_down):
    depth = w_in.shape[0]
    for l in range(depth):
        x = _layer(x, mem, norm1_g[l], w_in[l], conv_w[l], a_log[l], dt_bias[l], gdn_norm_g[l],
                   sb_q_norm_g[l], sb_k_norm_g[l], mem_norm_g[l], w_mem_kv[l], mem_q_norm_g[l],
                   mem_k_norm_g[l], w_br_gdn[l], w_br_sb[l], w_br_mem[l], w_o[l], norm2_g[l],
                   w_up[l], w_down[l])
    return x
```

```python
import functools

import jax
import jax.numpy as jnp
from jax import lax
from jax.experimental import pallas as pl
from jax.experimental.pallas import tpu as pltpu

F32 = jnp.float32
BF16 = jnp.bfloat16

EPS = 1e-6
N_HEAD = 4
HEAD_D = 128
BR_W = N_HEAD * HEAD_D
CONV_W = 4
CHUNK = 64
LANE = 128
V7X_VMEM_BYTES = 64 * 1024 * 1024
EXP_ZERO_BELOW = -104.0


def _vmem_limit(estimate_bytes):
    return int(min(estimate_bytes * 1.25 + (4 << 20), V7X_VMEM_BYTES - (6 << 20)))


def _dot(a, b):
    return jnp.dot(a, b, preferred_element_type=F32)


def _dot_nt(a, b):
    return lax.dot_general(a, b, (((1,), (1,)), ((), ())), preferred_element_type=F32)


def _dot_tn(a, b):
    return lax.dot_general(a, b, (((0,), (0,)), ((), ())), preferred_element_type=F32)


def _split(a):
    hi = a.astype(BF16)
    lo = (a - hi.astype(F32)).astype(BF16)
    return hi, lo


def _dot_exact_lhs(m_bf16, b):
    hi, lo = _split(b)
    return _dot(m_bf16, hi) + _dot(m_bf16, lo)


def _dot_exact_rhs(a, m_bf16):
    hi, lo = _split(a)
    return _dot(hi, m_bf16) + _dot(lo, m_bf16)


def _silu(x):
    return (0.5 * x) * (1.0 + jnp.tanh(0.5 * x))


def _sigmoid(x):
    return 0.5 + 0.5 * jnp.tanh(0.5 * x)


def _softplus(x):
    return jnp.maximum(x, 0.0) + jnp.log1p(jnp.exp(-jnp.abs(x)))


def _rms_rows(x, g_row):
    ms = jnp.mean(x * x, axis=-1, keepdims=True)
    return x * lax.rsqrt(ms + EPS) * g_row


IN_COL_CHUNK = 512
CONV_HALO = 8


def _in_proj_kernel(outs, n_w, tiles_per_seq, x_ref, g_ref, hg_ref, cw_ref, *refs):
    w_refs, out_refs, xp_ref = refs[:n_w], refs[n_w:n_w + len(outs)], refs[n_w + len(outs)]
    tm = x_ref.shape[0]

    @pl.when(pl.program_id(0) % tiles_per_seq == 0)
    def _():
        xp_ref[0:CONV_HALO, :] = jnp.zeros((CONV_HALO, xp_ref.shape[1]), F32)

    def conv_head(o_ref, h0):
        cols = slice(h0, h0 + HEAD_D)
        xe = xp_ref[:, cols]
        acc = None
        for j in range(CONV_W):
            shift = CONV_W - 1 - j
            rows = (pltpu.roll(xe, shift, axis=0) if shift else xe)[CONV_HALO:]
            term = cw_ref[j:j + 1, cols] * rows
            acc = term if acc is None else acc + term
        xp_ref[0:CONV_HALO, cols] = xe[tm:]
        a = _silu(acc)
        if h0 < 2 * BR_W:
            a = a * lax.rsqrt(jnp.sum(a * a, axis=-1, keepdims=True) + EPS)
        if h0 < BR_W:
            a = a * (HEAD_D ** -0.5)
        o_ref[:, cols] = a.astype(o_ref.dtype)

    hn = _rms_rows(x_ref[...], g_ref[...]).astype(BF16)
    for o_ref, (n, dtype, post, wi, wcol) in zip(out_refs, outs):
        for c0 in range(0, n, IN_COL_CHUNK):
            cw = min(IN_COL_CHUNK, n - c0)
            cols = slice(c0, c0 + cw)
            r = _dot(hn, w_refs[wi][:, wcol + c0:wcol + c0 + cw])
            if post == "gdn_qkv":
                xp_ref[CONV_HALO:CONV_HALO + tm, cols] = r
                for h0 in range(0, cw, HEAD_D):
                    conv_head(o_ref, c0 + h0)
                continue
            if post is None:
                o_ref[:, cols] = r.astype(dtype)
            elif post == "sigmoid":
                o_ref[:, cols] = _sigmoid(r).astype(dtype)
            else:
                gain = hg_ref[post:post + 1, :]
                for h0 in range(0, cw, HEAD_D):
                    o_ref[:, c0 + h0:c0 + h0 + HEAD_D] = _rms_rows(r[:, h0:h0 + HEAD_D], gain).astype(dtype)


def _in_proj(x2, g, head_gains, conv_w, ws, outs, tm, seq):
    t, d = x2.shape
    w3 = conv_w.shape[1]
    est = (2 * tm * d * 4 + sum(d * w.shape[1] * 2 for w in ws)
           + 2 * tm * sum(o[0] * jnp.dtype(o[1]).itemsize for o in outs)
           + (tm + CONV_HALO) * w3 * 4 + 8 * tm * IN_COL_CHUNK * 4)
    return pl.pallas_call(
        functools.partial(_in_proj_kernel, outs, len(ws), seq // tm),
        grid=(t // tm,),
        in_specs=[
            pl.BlockSpec((tm, d), lambda i: (i, 0)),
            pl.BlockSpec((1, d), lambda i: (0, 0)),
            pl.BlockSpec(head_gains.shape, lambda i: (0, 0)),
            pl.BlockSpec(conv_w.shape, lambda i: (0, 0)),
        ] + [pl.BlockSpec(w.shape, lambda i: (0, 0), pipeline_mode=pl.Buffered(1)) for w in ws],
        out_specs=[pl.BlockSpec((tm, o[0]), lambda i: (i, 0)) for o in outs],
        out_shape=[jax.ShapeDtypeStruct((t, o[0]), o[1]) for o in outs],
        scratch_shapes=[pltpu.VMEM((tm + CONV_HALO, w3), F32)],
        compiler_params=pltpu.CompilerParams(
            dimension_semantics=("arbitrary",), vmem_limit_bytes=_vmem_limit(est)),
        name="in_proj",
    )(x2, g, head_gains, conv_w, *ws)


def _dot3_pairs(a, b):
    m = a[0].shape[0]
    both = _dot(jnp.concatenate([a[0], a[1]], axis=0), b[0])
    return both[:m] + (both[m:] + _dot(a[0], b[1]))


def _pair_masks():
    row = lax.broadcasted_iota(jnp.int32, (CHUNK, 2 * CHUNK), 0)
    lane = lax.broadcasted_iota(jnp.int32, (CHUNK, 2 * CHUNK), 1)
    col = lane & (CHUNK - 1)
    return dict(first=lane < CHUNK, strict=row > col, causal=row >= col, eye=row == col,
                m4=(row >> 2) == (col >> 2), m8=(row >> 3) == (col >> 3), m16=(row >> 4) == (col >> 4), m32=(row >> 5) == (col >> 5))


def _block_diag(x, first):
    zero = jnp.zeros_like(x)
    return jnp.concatenate([jnp.where(first, x, zero), jnp.where(first, zero, x)], axis=0)


def _tri_inverse_pairs(lowers, mk):
    right = lambda hl: (_block_diag(hl[0], mk["first"]), _block_diag(hl[1], mk["first"]))
    eye = mk["eye"].astype(F32)
    c8 = mk["m8"] & jnp.logical_not(mk["m4"])
    c16 = mk["m16"] & jnp.logical_not(mk["m8"])
    c32 = mk["m32"] & jnp.logical_not(mk["m16"])
    ld = [jnp.where(mk["m4"], l, 0.0) for l in lowers]
    p = [eye - x for x in ld]
    a_l = [_split(x) for x in ld]
    a_r = [right(x) for x in a_l]
    a_r = [right(_split(_dot3_pairs(xl, xr))) for xl, xr in zip(a_l, a_r)]
    p = [pi + _dot3_pairs(_split(pi), xr) for pi, xr in zip(p, a_r)]
    for sel in (lambda l: jnp.where(c8, l, 0.0), lambda l: jnp.where(c16, l, 0.0),
                lambda l: jnp.where(c32, l, 0.0), lambda l: jnp.where(mk["m32"], 0.0, l)):
        p16 = [pi.astype(BF16) for pi in p]
        p_r = [_block_diag(x, mk["first"]) for x in p16]
        t = [_block_diag(_dot(sel(l).astype(BF16), xr).astype(BF16), mk["first"]) for l, xr in zip(lowers, p_r)]
        p = [pi - _dot(xl, ti) for pi, xl, ti in zip(p, p16, t)]
    return p


def _gdn_kernel(ts, q_s, k_s, v_s, z_ref, ab_ref, gp_ref, ng_ref, o_ref, g_s, b_s, state_ref):
    @pl.when(pl.program_id(1) == 0)
    def _():
        state_ref[...] = jnp.zeros_like(state_ref)

    ab = ab_ref[...]
    g_all = -jnp.exp(gp_ref[0:1, :]) * _softplus(ab + gp_ref[1:2, :])
    beta_all = _sigmoid(ab)
    for h in range(N_HEAD):
        sl = slice(h * HEAD_D, (h + 1) * HEAD_D)
        g_s[:, sl] = jnp.broadcast_to(g_all[:, h:h + 1], (ts, HEAD_D))
        b_s[:, sl] = jnp.broadcast_to(beta_all[:, N_HEAD + h:N_HEAD + h + 1], (ts, HEAD_D))

    mk = _pair_masks()
    tri = (lax.broadcasted_iota(jnp.int32, (CHUNK, CHUNK), 0)
           >= lax.broadcasted_iota(jnp.int32, (CHUNK, CHUNK), 1)).astype(BF16)
    ng = ng_ref[...]
    zero_blk = jnp.zeros((CHUNK, HEAD_D), BF16)

    def blk(ref, c, h):
        return ref[c * CHUNK:(c + 1) * CHUNK, h * HEAD_D:(h + 1) * HEAD_D]

    def block_diag2(a, b):
        return jnp.concatenate([jnp.concatenate([a, zero_blk], axis=1),
                                jnp.concatenate([zero_blk, b], axis=1)], axis=0)

    nc = ts // CHUNK
    probs = [(c, h) for c in range(nc) for h in range(0, N_HEAD, 2)]

    gcb = [_dot_exact_lhs(tri, g_s[c * CHUNK:(c + 1) * CHUNK, h * HEAD_D:(h + 2) * HEAD_D]) for c, h in probs]
    decay = [jnp.exp(_dot_exact_lhs(tri, jnp.where(
        mk["strict"], jnp.where(mk["first"], blk(g_s, c, h), blk(g_s, c, h + 1)), 0.0))) for c, h in probs]
    kk, qk = [], []
    for c, h in probs:
        kk2, qk2 = [], []
        for hh in (h, h + 1):
            k = blk(k_s, c, hh)
            k16 = k.astype(BF16)
            kk2.append(_dot_nt((k * blk(b_s, c, hh)).astype(BF16), k16))
            qk2.append(_dot_nt(blk(q_s, c, hh).astype(BF16), k16))
        kk.append(jnp.concatenate(kk2, axis=1))
        qk.append(jnp.concatenate(qk2, axis=1))
    lower = [jnp.where(mk["strict"], x * dc, 0.0) for x, dc in zip(kk, decay)]
    a_qk = [jnp.where(mk["causal"], x * dc, 0.0).astype(BF16) for x, dc in zip(qk, decay)]
    t_inv = [x.astype(BF16) for x in _tri_inverse_pairs(lower, mk)]
    wu, q_dec, k_dec, g_state = {}, {}, {}, {}
    for (c, h), ti, gc in zip(probs, t_inv, gcb):
        beta = [blk(b_s, c, hh) for hh in (h, h + 1)]
        eg = jnp.exp(gc)
        egs = (eg[:, :HEAD_D], eg[:, HEAD_D:])
        u2 = _dot(ti, block_diag2(*[(blk(v_s, c, hh) * bt).astype(BF16) for hh, bt in zip((h, h + 1), beta)]))
        w2 = _dot(ti, block_diag2(*[(blk(k_s, c, hh) * bt * e).astype(BF16)
                                    for hh, bt, e in zip((h, h + 1), beta, egs)]))
        for n, hh in enumerate((h, h + 1)):
            lanes = slice(n * HEAD_D, (n + 1) * HEAD_D)
            wu[c, hh] = jnp.concatenate([w2[:, lanes].astype(BF16), u2[:, lanes].astype(BF16)], axis=1)
            q_dec[c, hh] = blk(q_s, c, hh) * egs[n]
            g_last = gc[CHUNK - 1:CHUNK, lanes]
            k_dec[c, hh] = (blk(k_s, c, hh) * jnp.exp(g_last - gc[:, lanes])).astype(BF16)
            g_state[c, hh] = jnp.exp(g_last)
    k_wu = {p: _dot_tn(k_dec[p], wu[p]) for p in wu}
    zero_wu = jnp.zeros((CHUNK, 2 * HEAD_D), BF16)
    lhs, k_u, a_u = {}, {}, {}
    for (c, h), aq in zip(probs, a_qk):
        bd = jnp.concatenate([jnp.concatenate([wu[c, h], zero_wu], axis=1),
                              jnp.concatenate([zero_wu, wu[c, h + 1]], axis=1)], axis=0)
        a_wu = _dot(aq, bd)
        for n, hh in enumerate((h, h + 1)):
            a_w = a_wu[:, 2 * n * HEAD_D:(2 * n + 1) * HEAD_D]
            a_u[c, hh] = a_wu[:, (2 * n + 1) * HEAD_D:(2 * n + 2) * HEAD_D]
            k_u[c, hh] = k_wu[c, hh][:, HEAD_D:]
            lhs[c, hh] = jnp.concatenate([(-k_wu[c, hh][:, :HEAD_D]).astype(BF16),
                                          (q_dec[c, hh] - a_w).astype(BF16)], axis=0)

    state = [state_ref[h] for h in range(N_HEAD)]
    for c in range(nc):
        r = [_dot(lhs[c, h], state[h].astype(BF16)) for h in range(N_HEAD)]
        for h in range(N_HEAD):
            o = r[h][HEAD_D:] + a_u[c, h]
            rows = slice(c * CHUNK, (c + 1) * CHUNK)
            sl = slice(h * HEAD_D, (h + 1) * HEAD_D)
            o_ref[rows, sl] = (_rms_rows(o, ng) * _silu(z_ref[rows, sl])).astype(o_ref.dtype)
        state = [state[h] * g_state[c, h] + (k_u[c, h] + r[h][:HEAD_D]) for h in range(N_HEAD)]
    for h in range(N_HEAD):
        state_ref[h] = state[h]


def _gdn(gqkv, gz, gab, gp, ng, b, s, ts):
    nsb = s // ts
    est = (2 * ts * (4 * BR_W + LANE) * 4 + 2 * ts * BR_W * 2 + 2 * ts * BR_W * 4
           + N_HEAD * HEAD_D * HEAD_D * 4 + 24 * ts * BR_W * 4)
    tok = lambda n, j: pl.BlockSpec((ts, n), lambda bi, si: (bi * nsb + si, j))
    return pl.pallas_call(
        functools.partial(_gdn_kernel, ts),
        grid=(b, nsb),
        in_specs=[
            tok(BR_W, 0), tok(BR_W, 1), tok(BR_W, 2),
            tok(BR_W, 0), tok(LANE, 0),
            pl.BlockSpec((8, LANE), lambda bi, si: (0, 0)),
            pl.BlockSpec((1, HEAD_D), lambda bi, si: (0, 0)),
        ],
        out_specs=tok(BR_W, 0),
        out_shape=jax.ShapeDtypeStruct((b * s, BR_W), BF16),
        scratch_shapes=[
            pltpu.VMEM((ts, BR_W), F32),
            pltpu.VMEM((ts, BR_W), F32),
            pltpu.VMEM((N_HEAD, HEAD_D, HEAD_D), F32),
        ],
        compiler_params=pltpu.CompilerParams(
            dimension_semantics=("arbitrary", "arbitrary"), vmem_limit_bytes=_vmem_limit(est)),
        name="gdn",
    )(gqkv, gqkv, gqkv, gz, gab, gp, ng)


SB_BLK = 128
SB_NARROW = 32
LOG2E = 1.4426950408889634


def _sb_kernel(nsub, q_ref, k_ref, v_ref, o_ref, acc_ref, c_ref):
    i = pl.program_id(1)
    scale2 = (HEAD_D ** -0.5) * LOG2E
    dead_below = EXP_ZERO_BELOW * LOG2E
    row = lax.broadcasted_iota(jnp.int32, (SB_BLK, SB_BLK), 0)
    col = lax.broadcasted_iota(jnp.int32, (SB_BLK, SB_BLK), 1)
    before = col < row
    row2 = lax.broadcasted_iota(jnp.int32, (SB_BLK, 2 * SB_BLK), 0)
    col2 = lax.broadcasted_iota(jnp.int32, (SB_BLK, 2 * SB_BLK), 1)
    sum_rhs = jnp.logical_or(row2 > col2, col2 >= SB_BLK).astype(BF16)
    probs = [(j, h) for j in range(nsub) for h in range(N_HEAD)]
    lanes = lambda h: slice(h * HEAD_D, (h + 1) * HEAD_D)

    def step(t, diagonal, nrows):
        rows = lambda j: slice(j * SB_BLK, j * SB_BLK + nrows)
        k0s, zs = [], []
        for j, h in probs:
            kb = nsub * i + j - t
            k0 = pl.multiple_of(jnp.maximum(kb, 0) * SB_BLK, SB_BLK)
            z = _dot_nt(q_ref[rows(j), lanes(h)], k_ref[pl.ds(k0, SB_BLK), lanes(h)]) * scale2
            if not diagonal:
                z = z + jnp.where(kb >= 0, 0.0, -jnp.inf)
            k0s.append(k0)
            zs.append(z)
        log_1mb = [-(jnp.maximum(z, 0.0) + jnp.log2(1.0 + jnp.exp2(-jnp.abs(z)))) for z in zs]
        if diagonal:
            log_1mb = [jnp.where(before, x, 0.0) for x in log_1mb]
        sums = [_dot_exact_rhs(x, sum_rhs) for x in log_1mb]
        cmax = None
        for (j, h), k0, z, l1, sm in zip(probs, k0s, zs, log_1mb, sums):
            between = sm[:, :SB_BLK]
            c_new = sm[:, SB_BLK:]
            if not diagonal:
                c = c_ref[rows(j), lanes(h)]
                between = between + c
                c_new = c_new + c
            att = jnp.exp2(z + l1 + between)
            if diagonal:
                att = jnp.where(before, att, 0.0)
            pv = _dot(att.astype(BF16), v_ref[pl.ds(k0, SB_BLK), lanes(h)])
            if diagonal:
                acc_ref[rows(j), lanes(h)] = pv
            else:
                acc_ref[rows(j), lanes(h)] += pv
            c_ref[rows(j), lanes(h)] = c_new
            cmax = c_new if cmax is None else jnp.maximum(cmax, c_new)
        if nrows == SB_NARROW:
            return jnp.max(cmax), jnp.float32(-jnp.inf)
        return jnp.max(cmax[:SB_NARROW]), jnp.max(cmax[SB_NARROW:])

    def cond(carry):
        t, head, tail = carry
        return jnp.logical_and(t <= nsub * i + (nsub - 1), jnp.maximum(head, tail) > dead_below)

    def body(carry):
        t, _, tail = carry
        head, tail = lax.cond(tail > dead_below,
                              lambda: step(t, False, SB_BLK), lambda: step(t, False, SB_NARROW))
        return t + 1, head, tail

    lax.while_loop(cond, body, (jnp.int32(1),) + step(0, True, SB_BLK))
    o_ref[...] = acc_ref[...].astype(o_ref.dtype)


def _sb_attn(sq3, sk3, sv3, nsub):
    b, s, _ = sq3.shape
    tq = nsub * SB_BLK
    est = 4 * tq * BR_W * 2 + 4 * s * BR_W * 2 + 2 * tq * BR_W * 4 + 16 * nsub * N_HEAD * SB_BLK * SB_BLK * 4
    kv_spec = pl.BlockSpec((None, s, BR_W), lambda bi, i: (bi, 0, 0))
    return pl.pallas_call(
        functools.partial(_sb_kernel, nsub),
        grid=(b, s // tq),
        in_specs=[pl.BlockSpec((None, tq, BR_W), lambda bi, i: (bi, i, 0)), kv_spec, kv_spec],
        out_specs=pl.BlockSpec((None, tq, BR_W), lambda bi, i: (bi, i, 0)),
        out_shape=jax.ShapeDtypeStruct((b, s, BR_W), BF16),
        scratch_shapes=[pltpu.VMEM((tq, BR_W), F32), pltpu.VMEM((tq, BR_W), F32)],
        compiler_params=pltpu.CompilerParams(
            dimension_semantics=("arbitrary", "arbitrary"), vmem_limit_bytes=_vmem_limit(est)),
        name="sb_attn",
    )(sq3, sk3, sv3)


def _mem_kv_kernel(mem_ref, g_ref, w_ref, gk_ref, k_ref, v_ref):
    hn = _rms_rows(mem_ref[...], g_ref[...]).astype(BF16)
    kv = _dot(hn, w_ref[...])
    gk = gk_ref[...]
    for h in range(N_HEAD):
        sl = slice(h * HEAD_D, (h + 1) * HEAD_D)
        k_ref[:, sl] = _rms_rows(kv[:, sl], gk).astype(k_ref.dtype)
    v_ref[...] = kv[:, BR_W:].astype(v_ref.dtype)


def _mem_kv(mem, g, w, gk):
    b, m, d = mem.shape
    est = 2 * m * d * 4 + 2 * d * 2 * BR_W * 2 + 4 * m * BR_W * 2 + 4 * m * 2 * BR_W * 4
    return pl.pallas_call(
        _mem_kv_kernel,
        grid=(b,),
        in_specs=[
            pl.BlockSpec((None, m, d), lambda bi: (bi, 0, 0)),
            pl.BlockSpec((1, d), lambda bi: (0, 0)),
            pl.BlockSpec((d, 2 * BR_W), lambda bi: (0, 0)),
            pl.BlockSpec((1, HEAD_D), lambda bi: (0, 0)),
        ],
        out_specs=[pl.BlockSpec((None, m, BR_W), lambda bi: (bi, 0, 0))] * 2,
        out_shape=[jax.ShapeDtypeStruct((b, m, BR_W), BF16)] * 2,
        compiler_params=pltpu.CompilerParams(
            dimension_semantics=("arbitrary",), vmem_limit_bytes=_vmem_limit(est)),
        name="mem_kv",
    )(mem, g, w, gk)


def _mem_attn_kernel(q_ref, k_ref, v_ref, gq_ref, o_ref):
    scale = HEAD_D ** -0.5
    gq = gq_ref[...]
    for h in range(N_HEAD):
        sl = slice(h * HEAD_D, (h + 1) * HEAD_D)
        qn = _rms_rows(q_ref[:, sl], gq).astype(BF16)
        sc = _dot_nt(qn, k_ref[:, sl]) * scale
        e = jnp.exp(sc - jnp.max(sc, axis=-1, keepdims=True))
        o = _dot(e.astype(BF16), v_ref[:, sl])
        o_ref[:, sl] = (o / jnp.sum(e, axis=-1, keepdims=True)).astype(o_ref.dtype)


def _mem_attn(mq3, km, vm, gq, ts):
    b, s, _ = mq3.shape
    m = km.shape[1]
    est = 2 * ts * BR_W * 4 + 4 * m * BR_W * 2 + 2 * ts * BR_W * 2 + 6 * ts * m * 4
    return pl.pallas_call(
        _mem_attn_kernel,
        grid=(b, s // ts),
        in_specs=[
            pl.BlockSpec((None, ts, BR_W), lambda bi, si: (bi, si, 0)),
            pl.BlockSpec((None, m, BR_W), lambda bi, si: (bi, 0, 0)),
            pl.BlockSpec((None, m, BR_W), lambda bi, si: (bi, 0, 0)),
            pl.BlockSpec((1, HEAD_D), lambda bi, si: (0, 0)),
        ],
        out_specs=pl.BlockSpec((None, ts, BR_W), lambda bi, si: (bi, si, 0)),
        out_shape=jax.ShapeDtypeStruct((b, s, BR_W), BF16),
        compiler_params=pltpu.CompilerParams(
            dimension_semantics=("arbitrary", "arbitrary"), vmem_limit_bytes=_vmem_limit(est)),
        name="mem_attn",
    )(mq3, km, vm, gq)


FF_CHUNK = 1024


def _merge_mlp_kernel(x_ref, gate_ref, og_ref, os_ref, om_ref, wg_ref, ws_ref, wm_ref, wo_ref,
                      g2_ref, wu_ref, wd_ref, o_ref):
    d = x_ref.shape[1]
    mix = None
    for j, (b_ref, w_ref) in enumerate(((og_ref, wg_ref), (os_ref, ws_ref), (om_ref, wm_ref))):
        term = gate_ref[:, j * d:(j + 1) * d].astype(F32) * _dot(b_ref[...], w_ref[...])
        mix = term if mix is None else mix + term
    x1 = x_ref[...] + _dot(mix.astype(BF16), wo_ref[...])
    hn = _rms_rows(x1, g2_ref[...]).astype(BF16)
    acc = x1
    for c0 in range(0, wu_ref.shape[1], FF_CHUNK):
        up = _dot(hn, wu_ref[:, c0:c0 + FF_CHUNK])
        act = jnp.square(jnp.maximum(up, 0.0)).astype(BF16)
        acc = acc + _dot(act, wd_ref[c0:c0 + FF_CHUNK, :])
    o_ref[...] = acc


def _merge_mlp(x2, gates, og, osb, om, wg, ws, wm, wo, g2, wu, wd, tm):
    t, d = x2.shape
    ff = wu.shape[1]
    est = (2 * tm * d * 4 + 2 * tm * 3 * d * 2 + 6 * tm * BR_W * 2 + (3 * BR_W * d + d * d + 2 * d * ff) * 2
           + 2 * tm * d * 4 + 6 * tm * d * 4 + 3 * tm * FF_CHUNK * 4)
    wspec = lambda shape: pl.BlockSpec(shape, lambda i: (0, 0), pipeline_mode=pl.Buffered(1))
    tile = lambda n: pl.BlockSpec((tm, n), lambda i: (i, 0))
    return pl.pallas_call(
        _merge_mlp_kernel,
        grid=(t // tm,),
        in_specs=[
            tile(d), tile(3 * d), tile(BR_W), tile(BR_W), tile(BR_W),
            wspec((BR_W, d)), wspec((BR_W, d)), wspec((BR_W, d)), wspec((d, d)),
            pl.BlockSpec((1, d), lambda i: (0, 0)), wspec((d, ff)), wspec((ff, d)),
        ],
        out_specs=tile(d),
        out_shape=jax.ShapeDtypeStruct((t, d), F32),
        compiler_params=pltpu.CompilerParams(
            dimension_semantics=("arbitrary",), vmem_limit_bytes=_vmem_limit(est)),
        name="merge_mlp",
    )(x2, gates, og, osb, om, wg, ws, wm, wo, g2, wu, wd)


def _pick(n, pref):
    t = min(n, pref)
    while n % t:
        t //= 2
    return t


def _layer(x, mem, norm1_g, w_in, conv_w, a_log, dt_bias, gdn_norm_g, sb_q_norm_g, sb_k_norm_g,
           mem_norm_g, w_mem_kv, mem_q_norm_g, mem_k_norm_g, w_br_gdn, w_br_sb, w_br_mem, w_o,
           norm2_g, w_up, w_down):
    b, s, d = x.shape
    t = b * s
    x2 = x.reshape(t, d)
    row = lambda v: v.reshape(1, -1).astype(F32)

    n_ab = 2 * N_HEAD
    w_lo = w_in[:, :4 * BR_W].astype(BF16)
    w_ab = jnp.pad(w_in[:, 4 * BR_W:4 * BR_W + n_ab], ((0, 0), (0, LANE - n_ab))).astype(BF16)
    w_hi = w_in[:, 4 * BR_W + n_ab:].astype(BF16)
    outs = ((3 * d, BF16, "sigmoid", 2, 4 * BR_W), (3 * BR_W, F32, "gdn_qkv", 0, 0),
            (BR_W, BF16, 0, 2, 0), (BR_W, BF16, 1, 2, BR_W), (BR_W, BF16, None, 2, 2 * BR_W),
            (BR_W, F32, None, 0, 3 * BR_W), (BR_W, F32, None, 2, 3 * BR_W), (LANE, F32, None, 1, 0))
    head_gains = jnp.zeros((8, HEAD_D), F32).at[0].set(sb_q_norm_g.astype(F32)).at[1].set(sb_k_norm_g.astype(F32))
    gates, gqkv, sq, sk, sv, gz, mq, gab = _in_proj(
        x2, row(norm1_g), head_gains, conv_w.astype(F32), (w_lo, w_ab, w_hi), outs, _pick(s, 512), s)

    gp = jnp.zeros((8, LANE), F32)
    gp = gp.at[0, :N_HEAD].set(a_log.astype(F32)).at[1, :N_HEAD].set(dt_bias.astype(F32))
    o_gdn = _gdn(gqkv, gz, gab, gp, row(gdn_norm_g), b, s, _pick(s, 512))

    to3 = lambda a: a.reshape(b, s, BR_W)
    o_sb = _sb_attn(to3(sq), to3(sk), to3(sv), _pick(s // SB_BLK, 4))

    km, vm = _mem_kv(mem, row(mem_norm_g), w_mem_kv.astype(BF16), row(mem_k_norm_g))
    o_mem = _mem_attn(mq.reshape(b, s, BR_W), km, vm, row(mem_q_norm_g), _pick(s, 512))

    out = _merge_mlp(x2, gates, o_gdn, o_sb.reshape(t, BR_W), o_mem.reshape(t, BR_W),
                     w_br_gdn.astype(BF16), w_br_sb.astype(BF16), w_br_mem.astype(BF16),
                     w_o.astype(BF16), row(norm2_g), w_up.astype(BF16), w_down.astype(BF16), _pick(t, 512))
    return out.reshape(b, s, d)


def kernel(x, mem, norm1_g, w_in, conv_w, a_log, dt_bias, gdn_norm_g, sb_q_norm_g, sb_k_norm_g,
           mem_norm_g, w_mem_kv, mem_q_norm_g, mem_k_norm_g, w_br_gdn, w_br_sb, w_br_mem, w_o,
           norm2_g, w_up, w_down):
    depth = w_in.shape[0]
    for l in range(depth):
        x = _layer(x, mem, norm1_g[l], w_in[l], conv_w[l], a_log[l], dt_bias[l], gdn_norm_g[l],
                   sb_q_norm_g[l], sb_k_norm_g[l], mem_norm_g[l], w_mem_kv[l], mem_q_norm_g[l],
                   mem_k_norm_g[l], w_br_gdn[l], w_br_sb[l], w_br_mem[l], w_o[l], norm2_g[l],
                   w_up[l], w_down[l])
    return x
```
